```python
import math
import jax, jax.numpy as jnp
from jax import lax
import numpy as np

D_MODEL = 1024
BATCH = 8
SEQ = 2048
DEPTH = 2
DEC_BATCH = 32
DEC_SEQ = 1
PAST_LEN = 16384
PAGE_SIZE = 128

N_MIXERS = 2
N_A = (DEPTH + 1) // 2
N_B = DEPTH // 2

GDN_HEADS = 8
GDN_DK = 128
GDN_DV = 128
CONV_W = 4
GDN_CHUNK = 64
GDN_CONV_DIM = GDN_HEADS * (2 * GDN_DK + GDN_DV)
GDN_PROJ = GDN_CONV_DIM + GDN_HEADS * GDN_DV + 2 * GDN_HEADS

ATT_HEADS = 8
KV_HEADS = 4
GROUP = ATT_HEADS // KV_HEADS
HEAD_DIM = D_MODEL // ATT_HEADS
ROT_DIM = HEAD_DIM // 4
ROPE_THETA = 500000.0
MOBA_BLOCK = 256
MOBA_TOPK = 3
Q_CHUNK = 16
ATT_PROJ = (ATT_HEADS + 2 * KV_HEADS) * HEAD_DIM

D_FF = 3584
N_EXPERTS = 8
TOP_K = 2
EPS = 1e-6

kernel_name = "hybrid_gdn_moba_decode_step"


def rmsnorm(x, g):
    xf = x.astype(jnp.float32)
    y = xf * lax.rsqrt(jnp.mean(xf * xf, axis=-1, keepdims=True) + EPS)
    return (y * g.astype(jnp.float32)).astype(x.dtype)


def l2norm(x):
    return x * lax.rsqrt(jnp.sum(x * x, axis=-1, keepdims=True) + 1e-6)


def swiglu(h, w_gate, w_up, w_down):
    return (jax.nn.silu(h @ w_gate) * (h @ w_up)) @ w_down


def causal_short_conv(u, prev, w):
    L = u.shape[1]
    up = jnp.concatenate([prev.astype(jnp.float32), u], axis=1)
    wf = w.astype(jnp.float32)
    out = sum(up[:, j:j + L] * wf[j] for j in range(CONV_W))
    return jax.nn.silu(out), up[:, L:]


def _to_chunks(t, n, c):
    b, _, h = t.shape[:3]
    rest = t.shape[3:]
    t = t.reshape((b, n, c, h) + rest)
    return t.transpose((1, 0, 3, 2) + tuple(range(4, 4 + len(rest))))


def gated_delta_rule(q, k, v, g, beta, s0):
    b, L, h, dk = q.shape
    dv = v.shape[-1]
    c = math.gcd(L, GDN_CHUNK)
    n = L // c
    qc = _to_chunks(q * dk ** -0.5, n, c)
    kc = _to_chunks(k, n, c)
    vc = _to_chunks(v, n, c)
    gc = _to_chunks(g, n, c)
    bc = _to_chunks(beta, n, c)
    gam = jnp.cumsum(gc, axis=-1)
    tri = jnp.tril(jnp.ones((c, c), dtype=bool))
    strict = jnp.tril(jnp.ones((c, c), dtype=bool), -1)
    dmat = jnp.exp(jnp.where(tri, gam[..., :, None] - gam[..., None, :], -jnp.inf))
    kb = kc * bc[..., None]
    a = jnp.where(strict, jnp.einsum('nbhrk,nbhjk->nbhrj', kb, kc) * dmat, 0.0)
    m = a + jnp.eye(c, dtype=jnp.float32)
    rhs = jnp.concatenate([vc * bc[..., None], kb * jnp.exp(gam)[..., None]], axis=-1)
    sol = lax.linalg.triangular_solve(m, rhs, left_side=True, lower=True, unit_diagonal=True)
    u0, kcd = sol[..., :dv], sol[..., dv:]
    attn = jnp.einsum('nbhrk,nbhjk->nbhrj', qc, kc) * dmat
    q_dec = qc * jnp.exp(gam)[..., None]
    k_dec = kc * jnp.exp(gam[..., -1:] - gam)[..., None]
    g_end = jnp.exp(gam[..., -1])

    def step(s, xs):
        u0_i, kcd_i, attn_i, qd_i, kd_i, ge_i = xs
        u = u0_i - jnp.einsum('bhrk,bhkv->bhrv', kcd_i, s)
        o = jnp.einsum('bhrk,bhkv->bhrv', qd_i, s) + jnp.einsum('bhrj,bhjv->bhrv', attn_i, u)
        s = s * ge_i[..., None, None] + jnp.einsum('bhjk,bhjv->bhkv', kd_i, u)
        return s, o

    s_fin, o = lax.scan(step, s0, (u0, kcd, attn, q_dec, k_dec, g_end))
    o = o.transpose(1, 0, 3, 2, 4).reshape(b, L, h, dv)
    return o, s_fin


def gdn_mixer(h, conv_prev, s0, w_in, conv_w, a_log, dt_bias, o_gain, w_out):
    b, L, _ = h.shape
    hk = GDN_HEADS * GDN_DK
    hv = GDN_HEADS * GDN_DV
    proj = (h @ w_in).astype(jnp.float32)
    qkv, z, beta_raw, a_raw = jnp.split(
        proj, [GDN_CONV_DIM, GDN_CONV_DIM + hv, GDN_CONV_DIM + hv + GDN_HEADS], axis=-1)
    qkv, conv_state = causal_short_conv(qkv, conv_prev, conv_w)
    q, k, v = jnp.split(qkv, [hk, 2 * hk], axis=-1)
    q = l2norm(q.reshape(b, L, GDN_HEADS, GDN_DK))
    k = l2norm(k.reshape(b, L, GDN_HEADS, GDN_DK))
    v = v.reshape(b, L, GDN_HEADS, GDN_DV)
    beta = jax.nn.sigmoid(beta_raw)
    g = -jnp.exp(a_log.astype(jnp.float32)) * jax.nn.softplus(a_raw + dt_bias.astype(jnp.float32))
    o, s_fin = gated_delta_rule(q, k, v, g, beta, s0.astype(jnp.float32))
    o = rmsnorm(o, o_gain) * jax.nn.silu(z.reshape(b, L, GDN_HEADS, GDN_DV))
    y = o.reshape(b, L, hv).astype(h.dtype) @ w_out
    return y, conv_state.astype(h.dtype), s_fin


def partial_rope(x, pos):
    half = ROT_DIM // 2
    inv = jnp.power(ROPE_THETA, -jnp.arange(half, dtype=jnp.float32) * (2.0 / ROT_DIM))
    ang = pos.astype(jnp.float32)[:, None] * inv[None, :]
    cos = jnp.cos(ang)[None, :, None, :]
    sin = jnp.sin(ang)[None, :, None, :]
    x1, x2, rest = x[..., :half], x[..., half:ROT_DIM], x[..., ROT_DIM:]
    return jnp.concatenate([x1 * cos - x2 * sin, x2 * cos + x1 * sin, rest], axis=-1)


def moba_attend(q, k_all, v_all, q_pos):
    b, L = k_all.shape[:2]
    nq_tot = q.shape[1]
    nb = -(-L // MOBA_BLOCK)
    pad = ((0, 0), (0, nb * MOBA_BLOCK - L), (0, 0), (0, 0))
    kb = jnp.pad(k_all, pad).reshape(b, nb, MOBA_BLOCK, KV_HEADS, HEAD_DIM).transpose(0, 3, 1, 2, 4)
    vb = jnp.pad(v_all, pad).reshape(b, nb, MOBA_BLOCK, KV_HEADS, HEAD_DIM).transpose(0, 3, 1, 2, 4)
    k_mean = jnp.repeat(jnp.mean(kb, axis=3), GROUP, axis=1)
    ksel = min(MOBA_TOPK, nb)
    qc = math.gcd(nq_tot, Q_CHUNK)
    n_ch = nq_tot // qc
    q_ch = q.reshape(b, n_ch, qc, ATT_HEADS, HEAD_DIM).transpose(1, 0, 2, 3, 4)
    pos_ch = q_pos.reshape(n_ch, qc)
    b_idx = jnp.arange(b)[:, None, None, None]
    h_idx = (jnp.arange(ATT_HEADS) // GROUP)[None, None, :, None]
    scale = HEAD_DIM ** -0.5

    def one_chunk(args):
        qi, pi = args
        own = pi // MOBA_BLOCK
        gate = jnp.einsum('bqhd,bhnd->bqhn', qi, k_mean)
        eligible = jnp.arange(nb)[None, :] < own[:, None]
        gate = jnp.where(eligible[None, :, None, :], gate, -jnp.inf)
        _, top = lax.top_k(gate, ksel)
        sel_ok = jnp.arange(ksel)[None, :] < jnp.minimum(own, MOBA_TOPK)[:, None]
        blocks = jnp.concatenate(
            [top, jnp.broadcast_to(own[None, :, None, None], (b, qc, ATT_HEADS, 1))], axis=-1)
        ok = jnp.concatenate(
            [jnp.broadcast_to(sel_ok[None, :, None, :], (b, qc, ATT_HEADS, ksel)),
             jnp.ones((b, qc, ATT_HEADS, 1), dtype=bool)], axis=-1)
        kg = kb[b_idx, h_idx, blocks]
        vg = vb[b_idx, h_idx, blocks]
        kpos = blocks[..., None] * MOBA_BLOCK + jnp.arange(MOBA_BLOCK)
        mask = ok[..., None] & (kpos <= pi[None, :, None, None, None])
        logits = jnp.einsum('bqhd,bqhsjd->bqhsj', qi, kg) * scale
        logits = jnp.where(mask, logits, -jnp.inf).reshape(b, qc, ATT_HEADS, -1)
        p = jax.nn.softmax(logits, axis=-1).reshape(b, qc, ATT_HEADS, ksel + 1, MOBA_BLOCK)
        return jnp.einsum('bqhsj,bqhsjd->bqhd', p, vg)

    out = lax.map(one_chunk, (q_ch, pos_ch))
    return out.transpose(1, 0, 2, 3, 4).reshape(b, nq_tot, ATT_HEADS, HEAD_DIM)


def moba_mixer(h, pos, k_past, v_past, w_in, w_out):
    b, L, _ = h.shape
    nqd = ATT_HEADS * HEAD_DIM
    nkd = KV_HEADS * HEAD_DIM
    proj = (h @ w_in).astype(jnp.float32)
    q, k, v = jnp.split(proj, [nqd, nqd + nkd], axis=-1)
    q = partial_rope(q.reshape(b, L, ATT_HEADS, HEAD_DIM), pos)
    k = partial_rope(k.reshape(b, L, KV_HEADS, HEAD_DIM), pos)
    v = v.reshape(b, L, KV_HEADS, HEAD_DIM)
    if k_past is None:
        k_all, v_all = k, v
    else:
        k_all = jnp.concatenate([k_past.astype(jnp.float32), k], axis=1)
        v_all = jnp.concatenate([v_past.astype(jnp.float32), v], axis=1)
    o = moba_attend(q, k_all, v_all, pos)
    y = o.reshape(b, L, nqd).astype(h.dtype) @ w_out
    return y, k.astype(h.dtype), v.astype(h.dtype)


def moe_swiglu(h, w_router, w_gate, w_up, w_down):
    logits = (h @ w_router).astype(jnp.float32)
    top_val, top_idx = lax.top_k(logits, TOP_K)
    wts = jax.nn.softmax(top_val, axis=-1)
    gates = jnp.sum(jax.nn.one_hot(top_idx, N_EXPERTS, dtype=jnp.float32) * wts[..., None], axis=-2)
    y = jnp.zeros(h.shape, jnp.float32)
    for e in range(N_EXPERTS):
        y = y + gates[..., e:e + 1] * swiglu(h, w_gate[e], w_up[e], w_down[e]).astype(jnp.float32)
    return y.astype(h.dtype)


def setup_inputs(seed: int = 0) -> dict:
    key = jax.random.key(seed)
    ks = jax.random.split(key, 32)
    f32 = jnp.float32
    n_pages = PAST_LEN // PAGE_SIZE
    n_used = DEC_BATCH * n_pages
    n_pool = n_used + n_used // 4
    nrm = lambda k, s, sc: jax.random.normal(k, s, f32) * sc
    dt = jnp.exp(jax.random.uniform(ks[10], (N_A, GDN_HEADS), f32, math.log(1e-3), math.log(1e-1)))
    return {
        "x_prompt": nrm(ks[0], (BATCH, SEQ, D_MODEL), 1.0),
        "x_sample": nrm(ks[1], (DEC_BATCH, DEC_SEQ, D_MODEL), 1.0),
        "state_conv": nrm(ks[2], (N_A, DEC_BATCH, CONV_W - 1, GDN_CONV_DIM), 1.0),
        "state_delta": nrm(ks[3], (N_A, DEC_BATCH, GDN_HEADS, GDN_DK, GDN_DV), GDN_DK ** -0.5),
        "cache_k": nrm(ks[4], (N_B, n_pool, PAGE_SIZE, KV_HEADS, HEAD_DIM), 1.0),
        "cache_v": nrm(ks[5], (N_B, n_pool, PAGE_SIZE, KV_HEADS, HEAD_DIM), 1.0),
        "page_table": jax.random.permutation(ks[6], n_pool)[:n_used].reshape(DEC_BATCH, n_pages).astype(jnp.int32),
        "norm_mix": 1.0 + nrm(ks[7], (DEPTH, D_MODEL), 0.02),
        "norm_ffn": 1.0 + nrm(ks[8], (DEPTH, D_MODEL), 0.02),
        "norm_final": 1.0 + nrm(ks[9], (D_MODEL,), 0.02),
        "gdn_w_in": nrm(ks[11], (N_A, D_MODEL, GDN_PROJ), D_MODEL ** -0.5),
        "gdn_conv_w": nrm(ks[12], (N_A, CONV_W, GDN_CONV_DIM), CONV_W ** -0.5),
        "gdn_a_log": jnp.log(jax.random.uniform(ks[13], (N_A, GDN_HEADS), f32, 1.0, 16.0)),
        "gdn_dt_bias": dt + jnp.log(-jnp.expm1(-dt)),
        "gdn_o_norm": 1.0 + nrm(ks[14], (N_A, GDN_DV), 0.02),
        "gdn_w_out": nrm(ks[15], (N_A, GDN_HEADS * GDN_DV, D_MODEL), (GDN_HEADS * GDN_DV) ** -0.5),
        "att_w_in": nrm(ks[16], (N_B, D_MODEL, ATT_PROJ), D_MODEL ** -0.5),
        "att_w_out": nrm(ks[17], (N_B, ATT_HEADS * HEAD_DIM, D_MODEL), (ATT_HEADS * HEAD_DIM) ** -0.5),
        "ffn_w_gate": nrm(ks[18], (N_A, D_MODEL, D_FF), D_MODEL ** -0.5),
        "ffn_w_up": nrm(ks[19], (N_A, D_MODEL, D_FF), D_MODEL ** -0.5),
        "ffn_w_down": nrm(ks[20], (N_A, D_FF, D_MODEL), D_FF ** -0.5),
        "moe_router": nrm(ks[21], (N_B, D_MODEL, N_EXPERTS), D_MODEL ** -0.5),
        "moe_w_gate": nrm(ks[22], (N_B, N_EXPERTS, D_MODEL, D_FF), D_MODEL ** -0.5),
        "moe_w_up": nrm(ks[23], (N_B, N_EXPERTS, D_MODEL, D_FF), D_MODEL ** -0.5),
        "moe_w_down": nrm(ks[24], (N_B, N_EXPERTS, D_FF, D_MODEL), D_FF ** -0.5),
    }


def reference(x_prompt, x_sample, state_conv, state_delta, cache_k, cache_v, page_table,
              norm_mix, norm_ffn, norm_final,
              gdn_w_in, gdn_conv_w, gdn_a_log, gdn_dt_bias, gdn_o_norm, gdn_w_out,
              att_w_in, att_w_out,
              ffn_w_gate, ffn_w_up, ffn_w_down,
              moe_router, moe_w_gate, moe_w_up, moe_w_down):
    bp, seq, _ = x_prompt.shape
    db, dec_seq, _ = x_sample.shape
    n_pages = page_table.shape[1]
    past_len = n_pages * cache_k.shape[2]
    pos_p = jnp.arange(seq, dtype=jnp.int32)
    pos_s = past_len + jnp.arange(dec_seq, dtype=jnp.int32)
    hp, hs = x_prompt, x_sample
    conv_p, delta_p, k_p, v_p = [], [], [], []
    conv_s, delta_s, k_s, v_s = [], [], [], []
    for i in range(DEPTH):
        j = i // N_MIXERS
        if i % N_MIXERS == 0:
            zero_conv = jnp.zeros((bp, CONV_W - 1, GDN_CONV_DIM), hp.dtype)
            zero_s = jnp.zeros((bp, GDN_HEADS, GDN_DK, GDN_DV), jnp.float32)
            yp, cp, sp = gdn_mixer(rmsnorm(hp, norm_mix[i]), zero_conv, zero_s, gdn_w_in[j], gdn_conv_w[j],
                                   gdn_a_log[j], gdn_dt_bias[j], gdn_o_norm[j], gdn_w_out[j])
            ys, cs, ss = gdn_mixer(rmsnorm(hs, norm_mix[i]), state_conv[j], state_delta[j], gdn_w_in[j], gdn_conv_w[j],
                                   gdn_a_log[j], gdn_dt_bias[j], gdn_o_norm[j], gdn_w_out[j])
            hp = hp + yp
            hs = hs + ys
            conv_p.append(cp)
            delta_p.append(sp.astype(state_delta.dtype))
            conv_s.append(cs)
            delta_s.append(ss.astype(state_delta.dtype))
            hp = hp + swiglu(rmsnorm(hp, norm_ffn[i]), ffn_w_gate[j], ffn_w_up[j], ffn_w_down[j])
            hs = hs + swiglu(rmsnorm(hs, norm_ffn[i]), ffn_w_gate[j], ffn_w_up[j], ffn_w_down[j])
        else:
            k_past = cache_k[j][page_table].reshape(db, past_len, KV_HEADS, HEAD_DIM)
            v_past = cache_v[j][page_table].reshape(db, past_len, KV_HEADS, HEAD_DIM)
            yp, kp, vp = moba_mixer(rmsnorm(hp, norm_mix[i]), pos_p, None, None, att_w_in[j], att_w_out[j])
            ys, kn, vn = moba_mixer(rmsnorm(hs, norm_mix[i]), pos_s, k_past, v_past, att_w_in[j], att_w_out[j])
            hp = hp + yp
            hs = hs + ys
            k_p.append(kp)
            v_p.append(vp)
            k_s.append(kn)
            v_s.append(vn)
            hp = hp + moe_swiglu(rmsnorm(hp, norm_ffn[i]), moe_router[j], moe_w_gate[j], moe_w_up[j], moe_w_down[j])
            hs = hs + moe_swiglu(rmsnorm(hs, norm_ffn[i]), moe_router[j], moe_w_gate[j], moe_w_up[j], moe_w_down[j])
    y_prompt = rmsnorm(hp, norm_final)
    y_sample = rmsnorm(hs, norm_final)
    conv_prompt = jnp.stack(conv_p)
    delta_prompt = jnp.stack(delta_p)
    k_prompt = jnp.stack(k_p)
    v_prompt = jnp.stack(v_p)
    conv_sample = jnp.stack(conv_s)
    delta_sample = jnp.stack(delta_s)
    k_sample = jnp.stack(k_s)
    v_sample = jnp.stack(v_s)
    return (y_prompt, y_sample, conv_prompt, delta_prompt, k_prompt, v_prompt,
            conv_sample, delta_sample, k_sample, v_sample)
```

```python
import functools
import math

import jax
import jax.numpy as jnp
from jax import lax
from jax.experimental import pallas as pl
from jax.experimental.pallas import tpu as pltpu

F32 = jnp.float32
BF16 = jnp.bfloat16

GDN_HEADS = 8
GDN_DK = 128
GDN_DV = 128
CONV_W = 4
GDN_CHUNK = 64
ATT_HEADS = 8
KV_HEADS = 4
HEAD_DIM = 128
ROT_DIM = HEAD_DIM // 4
ROPE_THETA = 500000.0
MOBA_BLOCK = 256
MOBA_TOPK = 3
N_EXPERTS = 8
EPS = 1e-6

LANES = 128
SUBLANES = 8
VMEM_LIMIT = 56 * 1024 * 1024

NEG_INF = float("-inf")


def _params(*sem):
    return pltpu.CompilerParams(dimension_semantics=sem, vmem_limit_bytes=VMEM_LIMIT)


def _rms(x, gain):
    return x * lax.rsqrt(jnp.mean(x * x, axis=-1, keepdims=True) + EPS) * gain


def _silu(x):
    return x * jax.nn.sigmoid(x)


def _softplus(x):
    return jnp.maximum(x, 0.0) + jnp.log1p(jnp.exp(-jnp.abs(x)))


def _split2(a):
    hi = a.astype(BF16)
    lo = (a - hi.astype(F32)).astype(BF16)
    return hi, lo


def _split3(a):
    hi = a.astype(BF16)
    r = a - hi.astype(F32)
    mid = r.astype(BF16)
    lo = (r - mid.astype(F32)).astype(BF16)
    return hi, mid, lo


_NN = (((1,), (0,)), ((), ()))
_NT = (((1,), (1,)), ((), ()))
_TN = (((0,), (0,)), ((), ()))


def _dot(a, b, dims=_NN):
    return lax.dot_general(a, b, dims, preferred_element_type=F32)


def _dot_hp(a, b, dims=_NN):
    ah, al = _split2(a)
    bh, bl = _split2(b)
    return _dot(ah, bh, dims) + (_dot(al, bh, dims) + _dot(ah, bl, dims))


def _mm_kernel(*refs, has_gain, has_res, hp):
    it = iter(refs)
    a_ref = next(it)
    g_ref = next(it) if has_gain else None
    w_ref = next(it)
    r_ref = next(it) if has_res else None
    o_ref = next(it)
    an_ref = next(it)

    @pl.when(pl.program_id(1) == 0)
    def _():
        a = a_ref[...].astype(F32)
        if has_gain:
            a = _rms(a, g_ref[...])
        an_ref[...] = a.astype(an_ref.dtype)

    acc = (_dot_hp if hp else _dot)(an_ref[...], w_ref[...])
    if has_res:
        acc = acc + r_ref[...]
    o_ref[...] = acc.astype(o_ref.dtype)


def _matmul(a, w, *, gain=None, res=None, hp=False, tm=512, tn=None, out_dtype=F32, name="matmul"):
    m, k = a.shape
    n = w.shape[1]
    tm = min(tm, m)
    tn = n if tn is None else min(tn, n)
    assert m % tm == 0 and n % tn == 0
    args = [a]
    in_specs = [pl.BlockSpec((tm, k), lambda i, j: (i, 0))]
    if gain is not None:
        args.append(gain.reshape(1, k).astype(F32))
        in_specs.append(pl.BlockSpec((1, k), lambda i, j: (0, 0)))
    args.append(w if hp else w.astype(BF16))
    in_specs.append(pl.BlockSpec((k, tn), lambda i, j: (0, j)))
    if res is not None:
        args.append(res)
        in_specs.append(pl.BlockSpec((tm, tn), lambda i, j: (i, j)))
    scratch = [pltpu.VMEM((tm, k), F32 if hp else BF16)]
    return pl.pallas_call(
        functools.partial(_mm_kernel, has_gain=gain is not None, has_res=res is not None, hp=hp),
        out_shape=jax.ShapeDtypeStruct((m, n), out_dtype),
        grid=(m // tm, n // tn),
        in_specs=in_specs,
        out_specs=pl.BlockSpec((tm, tn), lambda i, j: (i, j)),
        scratch_shapes=scratch,
        compiler_params=_params("parallel", "arbitrary"),
        name=name,
    )(*args)


def _swiglu_step(xn, wg, wu, wd, hp):
    dot = _dot_hp if hp else _dot
    act = _silu(dot(xn, wg)) * dot(xn, wu)
    return dot(act if hp else act.astype(BF16), wd)


def _ffn_kernel(x_ref, g_ref, wg_ref, wu_ref, wd_ref, o_ref, xn_ref, acc_ref, *, hp):
    f = pl.program_id(1)

    @pl.when(f == 0)
    def _():
        xn_ref[...] = _rms(x_ref[...], g_ref[...]).astype(xn_ref.dtype)
        acc_ref[...] = jnp.zeros_like(acc_ref)

    acc_ref[...] += _swiglu_step(xn_ref[...], wg_ref[...], wu_ref[...], wd_ref[...], hp)

    @pl.when(f == pl.num_programs(1) - 1)
    def _():
        o_ref[...] = x_ref[...] + acc_ref[...]


def _ffn(x, gain, wg, wu, wd, *, hp=False, tm=1024, tf=512):
    m, d = x.shape
    dff = wg.shape[1]
    tm = min(tm, m)
    assert m % tm == 0 and dff % tf == 0
    wdt = F32 if hp else BF16
    return pl.pallas_call(
        functools.partial(_ffn_kernel, hp=hp),
        out_shape=jax.ShapeDtypeStruct((m, d), F32),
        grid=(m // tm, dff // tf),
        in_specs=[
            pl.BlockSpec((tm, d), lambda i, f: (i, 0)),
            pl.BlockSpec((1, d), lambda i, f: (0, 0)),
            pl.BlockSpec((d, tf), lambda i, f: (0, f)),
            pl.BlockSpec((d, tf), lambda i, f: (0, f)),
            pl.BlockSpec((tf, d), lambda i, f: (f, 0)),
        ],
        out_specs=pl.BlockSpec((tm, d), lambda i, f: (i, 0)),
        scratch_shapes=[pltpu.VMEM((tm, d), wdt), pltpu.VMEM((tm, d), F32)],
        compiler_params=_params("parallel", "arbitrary"),
        name="ffn",
    )(x, gain.reshape(1, d), wg.astype(wdt), wu.astype(wdt), wd.astype(wdt))


def _neumann_inverse(a, eye):
    c = a.shape[0]
    x = -a
    t = eye + x
    p = x
    for _ in range(int(math.log2(c)) - 1):
        p = _dot_hp(p, p)
        t = t + _dot_hp(t, p)
    return t


def _gdn_kernel(proj_ref, gate_ref, convw_ref, alog_ref, dtb_ref, ogain_ref,
                o_ref, sfin_ref, s_ref, tail_ref, *, tt, c):
    h_n, dk, dv = GDN_HEADS, GDN_DK, GDN_DV
    t = pl.program_id(1)

    @pl.when(t == 0)
    def _():
        s_ref[...] = jnp.zeros_like(s_ref)
        tail_ref[...] = jnp.zeros_like(tail_ref)

    gt = gate_ref[...]
    beta_all = jax.nn.sigmoid(gt)
    g_all = -jnp.exp(alog_ref[...]) * _softplus(gt + dtb_ref[...])

    ri = lax.broadcasted_iota(jnp.int32, (c, c), 0)
    ci = lax.broadcasted_iota(jnp.int32, (c, c), 1)
    tri = ri >= ci
    strict = ri > ci
    eye = (ri == ci).astype(F32)
    tri_b = tri.astype(BF16)
    ri2 = lax.broadcasted_iota(jnp.int32, (c, LANES), 0)
    ci2 = lax.broadcasted_iota(jnp.int32, (c, LANES), 1)
    upper_b = (ri2 <= ci2).astype(BF16)

    n_chunks = tt // c
    gams, gam_ts = [], []
    for ck in range(n_chunks):
        parts = _split3(g_all[ck * c:(ck + 1) * c, :])
        gam = sum(_dot(tri_b, p) for p in parts)
        gam_t = sum(_dot(p, upper_b, _TN) for p in parts)
        gams.append(gam)
        gam_ts.append(gam_t)

    def conv_silu(off):
        raw = proj_ref[:, off:off + LANES]
        xs = jnp.concatenate([tail_ref[:, off:off + LANES], raw], axis=0)
        w = convw_ref[:, off:off + LANES]
        acc = raw * w[CONV_W - 1:CONV_W]
        for j in range(CONV_W - 1):
            lo = SUBLANES - (CONV_W - 1) + j
            acc = acc + xs[lo:lo + tt] * w[j:j + 1]
        return _silu(acc)

    def l2n(x):
        return x * lax.rsqrt(jnp.sum(x * x, axis=-1, keepdims=True) + 1e-6)

    ogain = ogain_ref[...]
    for h in range(h_n):
        q = l2n(conv_silu(h * dk)) * (dk ** -0.5)
        k = l2n(conv_silu(h_n * dk + h * dk))
        v = conv_silu(2 * h_n * dk + h * dv)
        z = proj_ref[:, 2 * h_n * dk + h_n * dv + h * dv:2 * h_n * dk + h_n * dv + (h + 1) * dv]
        for ck in range(n_chunks):
            sl = slice(ck * c, (ck + 1) * c)
            qc, kc, vc = q[sl], k[sl], v[sl]
            bcol = beta_all[sl, h:h + 1]
            col = gams[ck][:, h_n + h:h_n + h + 1]
            row = gam_ts[ck][h_n + h:h_n + h + 1, :c]
            dm = jnp.exp(jnp.where(tri, col - row, NEG_INF))
            kb = kc * bcol
            kh = kc.astype(BF16)
            a = jnp.where(strict, _dot(kb.astype(BF16), kh, _NT) * dm, 0.0)
            tinv = _neumann_inverse(a, eye)
            eg = jnp.exp(col)
            rhs = jnp.concatenate([vc * bcol, kb * eg], axis=1)
            sol = _dot_hp(tinv, rhs)
            u0, kcd = sol[:, :dv], sol[:, dv:]
            attn = _dot(qc.astype(BF16), kh, _NT) * dm
            glast = col[c - 1:c, :]
            q_dec = qc * eg
            k_dec = kc * jnp.exp(glast - col)
            s = s_ref[h]
            sb = s.astype(BF16)
            both = _dot(jnp.concatenate([kcd, q_dec], axis=0).astype(BF16), sb)
            u = u0 - both[:c]
            ub = u.astype(BF16)
            o = both[c:] + _dot(attn.astype(BF16), ub)
            s_ref[h] = s * jnp.exp(glast) + _dot(k_dec.astype(BF16), ub, _TN)
            zc = z[sl]
            o_ref[sl, h * dv:(h + 1) * dv] = (_rms(o, ogain) * _silu(zc)).astype(o_ref.dtype)

    tail_ref[...] = proj_ref[tt - SUBLANES:tt, :tail_ref.shape[1]]

    @pl.when(t == pl.num_programs(1) - 1)
    def _():
        sfin_ref[0] = s_ref[...]


def _gdn_prompt(proj, gates, conv_w, alog_row, dtb_row, o_gain, *, batch, seq, tt=128):
    h_n, dk, dv = GDN_HEADS, GDN_DK, GDN_DV
    conv_dim = h_n * (2 * dk + dv)
    width = proj.shape[1]
    tt = min(tt, seq)
    c = math.gcd(seq, GDN_CHUNK)
    assert seq % tt == 0 and tt % c == 0 and tt >= SUBLANES
    nt = seq // tt
    return pl.pallas_call(
        functools.partial(_gdn_kernel, tt=tt, c=c),
        out_shape=(jax.ShapeDtypeStruct((batch * seq, h_n * dv), BF16),
                   jax.ShapeDtypeStruct((batch, h_n, dk, dv), F32)),
        grid=(batch, nt),
        in_specs=[
            pl.BlockSpec((tt, width), lambda b, t: (b * nt + t, 0)),
            pl.BlockSpec((tt, LANES), lambda b, t: (b * nt + t, 0)),
            pl.BlockSpec((CONV_W, conv_dim), lambda b, t: (0, 0)),
            pl.BlockSpec((1, LANES), lambda b, t: (0, 0)),
            pl.BlockSpec((1, LANES), lambda b, t: (0, 0)),
            pl.BlockSpec((1, dv), lambda b, t: (0, 0)),
        ],
        out_specs=(pl.BlockSpec((tt, h_n * dv), lambda b, t: (b * nt + t, 0)),
                   pl.BlockSpec((1, h_n, dk, dv), lambda b, t: (b, 0, 0, 0))),
        scratch_shapes=[pltpu.VMEM((h_n, dk, dv), F32), pltpu.VMEM((SUBLANES, conv_dim), F32)],
        compiler_params=_params("parallel", "arbitrary"),
        name="gdn_prompt",
    )(proj, gates, conv_w, alog_row, dtb_row, o_gain.reshape(1, dv))


def _gdn_step_kernel(proj_ref, gate_ref, cprev_ref, s_ref, convw_ref, alog_ref, dtb_ref, ogain_ref,
                     o_ref, cnew_ref, snew_ref):
    h_n, dk, dv = GDN_HEADS, GDN_DK, GDN_DV
    conv_dim = h_n * (2 * dk + dv)
    u_row = proj_ref[0]
    prev = cprev_ref[0]
    w = convw_ref[...]
    raw = u_row[:, :conv_dim]
    acc = raw * w[CONV_W - 1:CONV_W]
    for j in range(CONV_W - 1):
        acc = acc + prev[j:j + 1] * w[j:j + 1]
    qkv = _silu(acc)
    for j in range(CONV_W - 2):
        cnew_ref[0, j:j + 1, :] = prev[j + 1:j + 2]
    cnew_ref[0, CONV_W - 2:CONV_W - 1, :] = raw

    gt = gate_ref[0]
    beta_all = jax.nn.sigmoid(gt)
    g_all = -jnp.exp(alog_ref[...]) * _softplus(gt + dtb_ref[...])
    rows = lax.broadcasted_iota(jnp.int32, (SUBLANES, LANES), 0)
    ogain = ogain_ref[...]

    def l2n(x):
        return x * lax.rsqrt(jnp.sum(x * x, axis=-1, keepdims=True) + 1e-6)

    def as_row0(x, y=None):
        out = jnp.where(rows == 0, jnp.broadcast_to(x, (SUBLANES, LANES)), 0.0)
        if y is not None:
            out = jnp.where(rows == 1, jnp.broadcast_to(y, (SUBLANES, LANES)), out)
        return out

    for h in range(h_n):
        q = l2n(qkv[:, h * dk:(h + 1) * dk]) * (dk ** -0.5)
        k = l2n(qkv[:, h_n * dk + h * dk:h_n * dk + (h + 1) * dk])
        v = qkv[:, 2 * h_n * dk + h * dv:2 * h_n * dk + (h + 1) * dv]
        z = u_row[:, conv_dim + h * dv:conv_dim + (h + 1) * dv]
        beta = beta_all[:, h:h + 1]
        eg = jnp.exp(g_all[:, h_n + h:h_n + h + 1])
        s = s_ref[0, h]
        kcd = k * (beta * eg)
        q_dec = q * eg
        both = _dot_hp(as_row0(kcd, q_dec), s)
        u = v * beta - both[0:1]
        o = both[1:2] + jnp.sum(q * k, axis=-1, keepdims=True) * u
        snew_ref[0, h] = s * eg + _dot_hp(as_row0(k), as_row0(u), _TN)
        o_ref[0, :, h * dv:(h + 1) * dv] = (_rms(o, ogain) * _silu(z)).astype(o_ref.dtype)


def _gdn_step(proj, gates, conv_prev, s_prev, conv_w, alog_row, dtb_row, o_gain):
    h_n, dk, dv = GDN_HEADS, GDN_DK, GDN_DV
    conv_dim = h_n * (2 * dk + dv)
    b, width = proj.shape
    return pl.pallas_call(
        _gdn_step_kernel,
        out_shape=(jax.ShapeDtypeStruct((b, 1, h_n * dv), F32),
                   jax.ShapeDtypeStruct((b, CONV_W - 1, conv_dim), F32),
                   jax.ShapeDtypeStruct((b, h_n, dk, dv), F32)),
        grid=(b,),
        in_specs=[
            pl.BlockSpec((1, 1, width), lambda i: (i, 0, 0)),
            pl.BlockSpec((1, 1, LANES), lambda i: (i, 0, 0)),
            pl.BlockSpec((1, CONV_W - 1, conv_dim), lambda i: (i, 0, 0)),
            pl.BlockSpec((1, h_n, dk, dv), lambda i: (i, 0, 0, 0)),
            pl.BlockSpec((CONV_W, conv_dim), lambda i: (0, 0)),
            pl.BlockSpec((1, LANES), lambda i: (0, 0)),
            pl.BlockSpec((1, LANES), lambda i: (0, 0)),
            pl.BlockSpec((1, dv), lambda i: (0, 0)),
        ],
        out_specs=(pl.BlockSpec((1, 1, h_n * dv), lambda i: (i, 0, 0)),
                   pl.BlockSpec((1, CONV_W - 1, conv_dim), lambda i: (i, 0, 0)),
                   pl.BlockSpec((1, h_n, dk, dv), lambda i: (i, 0, 0, 0))),
        compiler_params=_params("parallel"),
        name="gdn_step",
    )(proj.reshape(b, 1, width), gates.reshape(b, 1, LANES), conv_prev, s_prev, conv_w,
      alog_row, dtb_row, o_gain.reshape(1, dv))


def _rope(x, cos, sin_a, sin_b):
    half = ROT_DIM // 2
    return (x * cos + pltpu.roll(x, HEAD_DIM - half, 1) * sin_a + pltpu.roll(x, half, 1) * sin_b)


def _rope_tables(pos):
    half = ROT_DIM // 2
    inv = jnp.power(ROPE_THETA, -jnp.arange(half, dtype=F32) * (2.0 / ROT_DIM))
    ang = pos.astype(F32)[:, None] * inv[None, :]
    cos, sin = jnp.cos(ang), jnp.sin(ang)
    n = pos.shape[0]
    rest = HEAD_DIM - ROT_DIM
    cos_t = jnp.concatenate([cos, cos, jnp.ones((n, rest), F32)], axis=1)
    sin_a = jnp.concatenate([-sin, jnp.zeros((n, half + rest), F32)], axis=1)
    sin_b = jnp.concatenate([jnp.zeros((n, half), F32), sin, jnp.zeros((n, rest), F32)], axis=1)
    return cos_t, sin_a, sin_b


def _moba_kernel(q_ref, k_ref, v_ref, cos_ref, sa_ref, sb_ref, o_ref, kout_ref, sel_ref, *, seq):
    blk, hd = MOBA_BLOCK, HEAD_DIM
    group = ATT_HEADS // KV_HEADS
    nb = seq // blk
    nq = group * blk
    scale = hd ** -0.5

    kout_ref[...] = _rope(k_ref[...], cos_ref[...], sa_ref[...], sb_ref[...])
    kmean = jnp.concatenate(
        [jnp.sum(kout_ref[n * blk:(n + 1) * blk, :], axis=0, keepdims=True) for n in range(nb)],
        axis=0) * (1.0 / blk)

    blk_id = lax.broadcasted_iota(jnp.int32, (nb, nq), 0)
    key_r = lax.broadcasted_iota(jnp.int32, (blk, nq), 0)
    qry_c = lax.broadcasted_iota(jnp.int32, (blk, nq), 1) % blk
    causal = key_r <= qry_c

    for i in range(nb):
        rows = slice(i * blk, (i + 1) * blk)
        cos, sa, sb = cos_ref[rows, :], sa_ref[rows, :], sb_ref[rows, :]
        qs = [_rope(q_ref[rows, g * hd:(g + 1) * hd], cos, sa, sb) for g in range(group)]
        q = jnp.concatenate(qs, axis=0)
        qb = q.astype(BF16)

        if i > MOBA_TOPK:
            gate = jnp.where(blk_id < i, _dot_hp(kmean, q, _NT), NEG_INF)
            rank = jnp.zeros((nb, nq), F32)
            for m in range(i):
                gm = gate[m:m + 1, :]
                rank = rank + jnp.where((gm > gate) | ((gm == gate) & (m < blk_id)), 1.0, 0.0)
            sel_ref[...] = jnp.where((blk_id < i) & (rank < MOBA_TOPK), 1.0, 0.0)
        else:
            sel_ref[...] = jnp.where(blk_id < i, 1.0, 0.0)

        def key_rows(j):
            if isinstance(j, int):
                return slice(j * blk, (j + 1) * blk)
            return pl.ds(pl.multiple_of(j * blk, blk), blk)

        def scores(j):
            kj = kout_ref[key_rows(j), :].astype(BF16)
            return _dot(kj, qb, _NT) * scale

        def accumulate(j, s, carry):
            m_run, l_run, acc = carry
            m_new = jnp.maximum(m_run, jnp.max(s, axis=0, keepdims=True))
            m_safe = jnp.where(m_new == NEG_INF, 0.0, m_new)
            p = jnp.exp(s - m_safe)
            alpha = jnp.exp(m_run - m_safe)
            vj = v_ref[key_rows(j), :].astype(BF16)
            acc = alpha * acc + _dot(vj, p.astype(BF16), _TN)
            return m_new, alpha * l_run + jnp.sum(p, axis=0, keepdims=True), acc

        def past_block(j, carry):
            s = jnp.where(sel_ref[pl.ds(j, 1), :] > 0.0, scores(j), NEG_INF)
            return accumulate(j, s, carry)

        carry = (jnp.full((1, nq), NEG_INF, F32), jnp.zeros((1, nq), F32), jnp.zeros((hd, nq), F32))
        if i > 0:
            carry = lax.fori_loop(0, i, past_block, carry)
        _, l_fin, acc = accumulate(i, jnp.where(causal, scores(i), NEG_INF), carry)
        out = (acc / l_fin).T
        for g in range(group):
            o_ref[rows, g * hd:(g + 1) * hd] = out[g * blk:(g + 1) * blk].astype(o_ref.dtype)


def _moba_prompt(proj, cos_t, sin_a, sin_b, *, batch, seq):
    hd = HEAD_DIM
    group = ATT_HEADS // KV_HEADS
    assert seq % MOBA_BLOCK == 0
    nb = seq // MOBA_BLOCK
    tbl = pl.BlockSpec((seq, hd), lambda b, g: (0, 0))
    return pl.pallas_call(
        functools.partial(_moba_kernel, seq=seq),
        out_shape=(jax.ShapeDtypeStruct((batch * seq, ATT_HEADS * hd), BF16),
                   jax.ShapeDtypeStruct((batch * seq, KV_HEADS * hd), F32)),
        grid=(batch, KV_HEADS),
        in_specs=[
            pl.BlockSpec((seq, group * hd), lambda b, g: (b, g)),
            pl.BlockSpec((seq, hd), lambda b, g: (b, ATT_HEADS + g)),
            pl.BlockSpec((seq, hd), lambda b, g: (b, ATT_HEADS + KV_HEADS + g)),
            tbl, tbl, tbl,
        ],
        out_specs=(pl.BlockSpec((seq, group * hd), lambda b, g: (b, g)),
                   pl.BlockSpec((seq, hd), lambda b, g: (b, g))),
        scratch_shapes=[pltpu.VMEM((nb, group * MOBA_BLOCK), F32)],
        compiler_params=_params("parallel", "parallel"),
        name="moba_prompt",
    )(proj, proj, proj, cos_t, sin_a, sin_b)


def _kmean_kernel(pt_ref, *refs, pages, pages_per_block):
    page_refs, o_ref = refs[:pages], refs[pages]
    rows = page_refs[0].shape[1]
    tokens = pages_per_block * rows // KV_HEADS
    for n in range(pages // pages_per_block):
        part = jnp.zeros((SUBLANES, HEAD_DIM), F32)
        for p in range(pages_per_block):
            x = page_refs[n * pages_per_block + p][0]
            part = part + jnp.sum(x.reshape(rows // SUBLANES, SUBLANES, HEAD_DIM), axis=0)
        acc = part[:KV_HEADS]
        for r in range(1, SUBLANES // KV_HEADS):
            acc = acc + part[r * KV_HEADS:(r + 1) * KV_HEADS]
        o_ref[0, n] = acc * (1.0 / tokens)


def _cache_block_means(cache, page_table, *, blocks_per_step=8):
    n_pool, rows, hd = cache.shape
    page = rows // KV_HEADS
    b, n_pages = page_table.shape
    ppb = MOBA_BLOCK // page
    assert ppb * page == MOBA_BLOCK and n_pages % ppb == 0 and rows % SUBLANES == 0
    n_blocks = n_pages // ppb
    bps = math.gcd(blocks_per_step, n_blocks)
    pages = bps * ppb
    specs = [pl.BlockSpec((1, rows, hd), lambda i, s, pt, p=p: (pt[i, s * pages + p], 0, 0))
             for p in range(pages)]
    return pl.pallas_call(
        functools.partial(_kmean_kernel, pages=pages, pages_per_block=ppb),
        out_shape=jax.ShapeDtypeStruct((b, n_blocks, KV_HEADS, hd), F32),
        grid_spec=pltpu.PrefetchScalarGridSpec(
            num_scalar_prefetch=1,
            grid=(b, n_blocks // bps),
            in_specs=specs,
            out_specs=pl.BlockSpec((1, bps, KV_HEADS, hd), lambda i, s, pt: (i, s, 0, 0)),
        ),
        compiler_params=_params("parallel", "arbitrary"),
        name="cache_block_means",
    )(page_table, *([cache] * pages))


def _moba_select_kernel(proj_ref, km_ref, cos_ref, sa_ref, sb_ref, q_ref, k_ref, sel_ref, *, n_sel):
    hd = HEAD_DIM
    group = ATT_HEADS // KV_HEADS
    row = proj_ref[0]
    cos, sa, sb = cos_ref[...], sa_ref[...], sb_ref[...]
    rows = lax.broadcasted_iota(jnp.int32, (SUBLANES, hd), 0)
    q = jnp.zeros((SUBLANES, hd), F32)
    for h in range(ATT_HEADS):
        qh = _rope(row[:, h * hd:(h + 1) * hd], cos, sa, sb)
        q = jnp.where(rows == h, jnp.broadcast_to(qh, (SUBLANES, hd)), q)
    q_ref[0] = q
    for g in range(KV_HEADS):
        off = (ATT_HEADS + g) * hd
        k_ref[0, :, g * hd:(g + 1) * hd] = _rope(row[:, off:off + hd], cos, sa, sb)

    nb = km_ref.shape[1]
    hrow = lax.broadcasted_iota(jnp.int32, (ATT_HEADS, nb), 0)
    lane = lax.broadcasted_iota(jnp.int32, (ATT_HEADS, nb), 1)
    gate = jnp.zeros((ATT_HEADS, nb), F32)
    for g in range(KV_HEADS):
        gg = _dot_hp(q, km_ref[0, :, g, :], _NT)
        gate = jnp.where(hrow // group == g, gg, gate)
    out_lane = lax.broadcasted_iota(jnp.int32, (ATT_HEADS, LANES), 1)
    sel = jnp.zeros((ATT_HEADS, LANES), jnp.int32)
    for t in range(n_sel):
        m = jnp.max(gate, axis=-1, keepdims=True)
        idx = jnp.min(jnp.where(gate == m, lane, nb), axis=-1, keepdims=True)
        sel = jnp.where(out_lane == t, idx, sel)
        gate = jnp.where(lane == idx, NEG_INF, gate)
    sel_ref[0] = sel


def _moba_select(proj, kmeans, cos_t, sin_a, sin_b, *, n_sel):
    b, width = proj.shape
    nb = kmeans.shape[1]
    hd = HEAD_DIM
    tbl = pl.BlockSpec((1, hd), lambda i: (0, 0))
    return pl.pallas_call(
        functools.partial(_moba_select_kernel, n_sel=n_sel),
        out_shape=(jax.ShapeDtypeStruct((b, ATT_HEADS, hd), F32),
                   jax.ShapeDtypeStruct((b, 1, KV_HEADS * hd), F32),
                   jax.ShapeDtypeStruct((b, ATT_HEADS, LANES), jnp.int32)),
        grid=(b,),
        in_specs=[pl.BlockSpec((1, 1, width), lambda i: (i, 0, 0)),
                  pl.BlockSpec((1, nb, KV_HEADS, hd), lambda i: (i, 0, 0, 0)),
                  tbl, tbl, tbl],
        out_specs=(pl.BlockSpec((1, ATT_HEADS, hd), lambda i: (i, 0, 0)),
                   pl.BlockSpec((1, 1, KV_HEADS * hd), lambda i: (i, 0, 0)),
                   pl.BlockSpec((1, ATT_HEADS, LANES), lambda i: (i, 0, 0))),
        compiler_params=_params("parallel"),
        name="moba_select",
    )(proj.reshape(b, 1, width), kmeans, cos_t, sin_a, sin_b)


def _moba_step_kernel(pg_ref, q_ref, kn_ref, vn_ref, *refs, n_pages):
    k_refs, v_refs, o_ref = refs[:n_pages], refs[n_pages:2 * n_pages], refs[2 * n_pages]
    hd = HEAD_DIM
    group = ATT_HEADS // KV_HEADS
    scale = hd ** -0.5
    h = pl.program_id(1)
    g = h // group
    rows = k_refs[0].shape[1]
    q = q_ref[0, pl.ds(h, 1), :]
    q8 = jnp.broadcast_to(q, (SUBLANES, hd))
    lane_kvh = lax.broadcasted_iota(jnp.int32, (1, rows), 1) % KV_HEADS
    mine = lane_kvh == g
    kn = kn_ref[0, pl.ds(g, 1), :]
    vn = vn_ref[0, pl.ds(g, 1), :]
    s_self = jnp.sum(q * kn, axis=-1, keepdims=True) * scale
    scores = []
    m = s_self
    for p in range(n_pages):
        s = _dot_hp(q8, k_refs[p][0], _NT)[0:1] * scale
        s = jnp.where(mine, s, NEG_INF)
        scores.append(s)
        m = jnp.maximum(m, jnp.max(s, axis=-1, keepdims=True))
    p_self = jnp.exp(s_self - m)
    l = p_self
    acc = p_self * vn
    for p in range(n_pages):
        e = jnp.exp(scores[p] - m)
        l = l + jnp.sum(e, axis=-1, keepdims=True)
        acc = acc + _dot_hp(jnp.broadcast_to(e, (SUBLANES, rows)), v_refs[p][0])[0:1]
    o_ref[0] = (acc / l).astype(o_ref.dtype)


def _moba_step(q, k_new, v_new, cache_k, cache_v, pages):
    b = q.shape[0]
    hd = HEAD_DIM
    rows = cache_k.shape[1]
    n_pages = pages.shape[0] // (b * ATT_HEADS)

    def page_spec(p):
        return pl.BlockSpec((1, rows, hd), lambda i, h, pg, p=p: (pg[(i * ATT_HEADS + h) * n_pages + p], 0, 0))

    specs = ([pl.BlockSpec((1, ATT_HEADS, hd), lambda i, h, pg: (i, 0, 0)),
              pl.BlockSpec((1, KV_HEADS, hd), lambda i, h, pg: (i, 0, 0)),
              pl.BlockSpec((1, KV_HEADS, hd), lambda i, h, pg: (i, 0, 0))]
             + [page_spec(p) for p in range(n_pages)] * 2)
    out = pl.pallas_call(
        functools.partial(_moba_step_kernel, n_pages=n_pages),
        out_shape=jax.ShapeDtypeStruct((b * ATT_HEADS, 1, hd), F32),
        grid_spec=pltpu.PrefetchScalarGridSpec(
            num_scalar_prefetch=1,
            grid=(b, ATT_HEADS),
            in_specs=specs,
            out_specs=pl.BlockSpec((1, 1, hd), lambda i, h, pg: (i * ATT_HEADS + h, 0, 0)),
        ),
        compiler_params=_params("parallel", "arbitrary"),
        name="moba_step",
    )(pages, q, k_new, v_new, *([cache_k] * n_pages), *([cache_v] * n_pages))
    return out.reshape(b, ATT_HEADS * hd)


def _router_kernel(x_ref, g_ref, w_ref, idx_ref, wts_ref):
    logits = _dot_hp(_rms(x_ref[...], g_ref[...]), w_ref[...])
    lane = lax.broadcasted_iota(jnp.int32, logits.shape, 1)
    logits = jnp.where(lane < N_EXPERTS, logits, NEG_INF)
    m1 = jnp.max(logits, axis=-1, keepdims=True)
    i1 = jnp.min(jnp.where(logits == m1, lane, LANES), axis=-1, keepdims=True)
    rest = jnp.where(lane == i1, NEG_INF, logits)
    m2 = jnp.max(rest, axis=-1, keepdims=True)
    i2 = jnp.min(jnp.where(rest == m2, lane, LANES), axis=-1, keepdims=True)
    e = jnp.exp(m2 - m1)
    w1 = 1.0 / (1.0 + e)
    idx_ref[...] = jnp.where(lane == 0, i1, jnp.where(lane == 1, i2, 0))
    wts_ref[...] = jnp.where(lane == 0, w1, jnp.where(lane == 1, e * w1, 0.0))


def _router(x, gain, w_router, *, tm=512):
    m, d = x.shape
    tm = min(tm, m)
    assert m % tm == 0
    w = jnp.pad(w_router, ((0, 0), (0, LANES - w_router.shape[1])))
    return pl.pallas_call(
        _router_kernel,
        out_shape=(jax.ShapeDtypeStruct((m, LANES), jnp.int32), jax.ShapeDtypeStruct((m, LANES), F32)),
        grid=(m // tm,),
        in_specs=[pl.BlockSpec((tm, d), lambda i: (i, 0)), pl.BlockSpec((1, d), lambda i: (0, 0)),
                  pl.BlockSpec((d, LANES), lambda i: (0, 0))],
        out_specs=(pl.BlockSpec((tm, LANES), lambda i: (i, 0)), pl.BlockSpec((tm, LANES), lambda i: (i, 0))),
        compiler_params=_params("parallel"),
        name="router",
    )(x, gain.reshape(1, d), w)


def _row_copy(src, dst, sem, i, j):
    return pltpu.make_async_copy(src.at[pl.ds(i, 1)], dst.at[pl.ds(j, 1)], sem)


def _gather_kernel(idx_ref, src_ref, out_ref, sem, *, rows):
    base = pl.program_id(0) * rows

    def issue(r, carry):
        _row_copy(src_ref, out_ref, sem, idx_ref[base + r], base + r).start()
        return carry

    def drain(r, carry):
        _row_copy(src_ref, out_ref, sem, 0, base + r).wait()
        return carry

    lax.fori_loop(0, rows, issue, 0)
    lax.fori_loop(0, rows, drain, 0)


def _gather_rows(src, idx, *, rows=512):
    n = idx.shape[0]
    rows = math.gcd(rows, n)
    return pl.pallas_call(
        functools.partial(_gather_kernel, rows=rows),
        out_shape=jax.ShapeDtypeStruct((n,) + src.shape[1:], src.dtype),
        grid_spec=pltpu.PrefetchScalarGridSpec(
            num_scalar_prefetch=1,
            grid=(n // rows,),
            in_specs=[pl.BlockSpec(memory_space=pl.ANY)],
            out_specs=pl.BlockSpec(memory_space=pl.ANY),
            scratch_shapes=[pltpu.SemaphoreType.DMA],
        ),
        compiler_params=_params("arbitrary"),
        name="gather_rows",
    )(idx, src)


def _moe_ffn_kernel(te_ref, na_ref, x_ref, g_ref, wg_ref, wu_ref, wd_ref, o_ref, xn_ref, acc_ref, *, hp):
    t, f = pl.program_id(0), pl.program_id(1)
    last = f == pl.num_programs(1) - 1
    active = t < na_ref[0]

    @pl.when(active)
    def _():
        @pl.when(f == 0)
        def _():
            xn_ref[...] = _rms(x_ref[...], g_ref[...]).astype(xn_ref.dtype)
            acc_ref[...] = jnp.zeros_like(acc_ref)

        acc_ref[...] += _swiglu_step(xn_ref[...], wg_ref[0], wu_ref[0], wd_ref[0], hp)

        @pl.when(last)
        def _():
            o_ref[...] = acc_ref[...]

    @pl.when(jnp.logical_and(jnp.logical_not(active), last))
    def _():
        o_ref[...] = jnp.zeros_like(o_ref)


def _moe_ffn(xs, gain, wg, wu, wd, tile_expert, n_active, *, tm, hp=False, tf=512):
    s, d = xs.shape
    dff = wg.shape[2]
    assert s % tm == 0 and dff % tf == 0
    wdt = F32 if hp else BF16
    return pl.pallas_call(
        functools.partial(_moe_ffn_kernel, hp=hp),
        out_shape=jax.ShapeDtypeStruct((s, d), F32),
        grid_spec=pltpu.PrefetchScalarGridSpec(
            num_scalar_prefetch=2,
            grid=(s // tm, dff // tf),
            in_specs=[
                pl.BlockSpec((tm, d), lambda t, f, te, na: (t, 0)),
                pl.BlockSpec((1, d), lambda t, f, te, na: (0, 0)),
                pl.BlockSpec((1, d, tf), lambda t, f, te, na: (te[t], 0, f)),
                pl.BlockSpec((1, d, tf), lambda t, f, te, na: (te[t], 0, f)),
                pl.BlockSpec((1, tf, d), lambda t, f, te, na: (te[t], f, 0)),
            ],
            out_specs=pl.BlockSpec((tm, d), lambda t, f, te, na: (t, 0)),
            scratch_shapes=[pltpu.VMEM((tm, d), wdt), pltpu.VMEM((tm, d), F32)],
        ),
        compiler_params=_params("parallel", "arbitrary"),
        name="moe_ffn",
    )(tile_expert, n_active, xs, gain.reshape(1, d), wg.astype(wdt), wu.astype(wdt), wd.astype(wdt))


def _combine_kernel(slot_ref, x_ref, wts_ref, *refs, tc, has_norm):
    if has_norm:
        gain_ref, y_ref, o_ref, buf_ref, sem = refs
    else:
        y_ref, o_ref, buf_ref, sem = refs
    base = pl.program_id(0) * tc

    def issue(r, carry):
        for k in range(2):
            _row_copy(y_ref, buf_ref.at[k], sem, slot_ref[2 * (base + r) + k], r).start()
        return carry

    def drain(r, carry):
        for k in range(2):
            _row_copy(y_ref, buf_ref.at[k], sem, 0, r).wait()
        return carry

    lax.fori_loop(0, tc, issue, 0)
    lax.fori_loop(0, tc, drain, 0)
    w = wts_ref[...]
    out = x_ref[...] + (w[:, 0:1] * buf_ref[0] + w[:, 1:2] * buf_ref[1])
    if has_norm:
        out = _rms(out, gain_ref[...])
    o_ref[...] = out


def _combine(x, wts, y_sorted, slots, final_gain, *, tc=256):
    m, d = x.shape
    tc = min(tc, m)
    assert m % tc == 0
    has_norm = final_gain is not None
    args = [slots, x, wts]
    in_specs = [pl.BlockSpec((tc, d), lambda i, sl: (i, 0)), pl.BlockSpec((tc, LANES), lambda i, sl: (i, 0))]
    if has_norm:
        args.append(final_gain.reshape(1, d))
        in_specs.append(pl.BlockSpec((1, d), lambda i, sl: (0, 0)))
    args.append(y_sorted)
    in_specs.append(pl.BlockSpec(memory_space=pl.ANY))
    return pl.pallas_call(
        functools.partial(_combine_kernel, tc=tc, has_norm=has_norm),
        out_shape=jax.ShapeDtypeStruct((m, d), F32),
        grid_spec=pltpu.PrefetchScalarGridSpec(
            num_scalar_prefetch=1,
            grid=(m // tc,),
            in_specs=in_specs,
            out_specs=pl.BlockSpec((tc, d), lambda i, sl: (i, 0)),
            scratch_shapes=[pltpu.VMEM((2, tc, d), F32), pltpu.SemaphoreType.DMA],
        ),
        compiler_params=_params("arbitrary"),
        name="moe_combine",
    )(*args)


def _moe_layer(x, gain, w_router, wg, wu, wd, final_gain, *, tm, hp=False):
    m, d = x.shape
    n_e = wg.shape[0]
    idx, wts = _router(x, gain, w_router)
    e_flat = idx[:, :2].reshape(-1)
    onehot = (e_flat[:, None] == jnp.arange(n_e, dtype=jnp.int32)[None, :]).astype(jnp.int32)
    csum = jnp.cumsum(onehot, axis=0)
    rank = jnp.sum(csum * onehot, axis=1) - 1
    tiles_e = (csum[-1] + tm - 1) // tm
    tile_end = jnp.cumsum(tiles_e)
    slot = ((tile_end - tiles_e)[e_flat] * tm + rank).astype(jnp.int32)
    n_tiles = (2 * m) // tm + n_e
    tile_expert = jnp.minimum(
        jnp.sum(jnp.arange(n_tiles, dtype=jnp.int32)[:, None] >= tile_end[None, :], axis=1), n_e - 1).astype(jnp.int32)
    n_active = tile_end[-1:].astype(jnp.int32)
    src_token = jnp.zeros((n_tiles * tm,), jnp.int32).at[slot].set(jnp.arange(2 * m, dtype=jnp.int32) // 2)
    xs = _gather_rows(x, src_token)
    ys = _moe_ffn(xs, gain, wg, wu, wd, tile_expert, n_active, tm=tm, hp=hp)
    return _combine(x, wts, ys, slot, final_gain)


def _gdn_weights(w_in, a_log, dt_bias):
    h_n = GDN_HEADS
    main = 2 * h_n * GDN_DK + 2 * h_n * GDN_DV
    w_gate = jnp.pad(w_in[:, main:], ((0, 0), (0, LANES - 2 * h_n)))
    pad = (h_n, LANES - 2 * h_n)
    alog_row = jnp.pad(a_log.astype(F32), pad).reshape(1, LANES)
    dtb_row = jnp.pad(dt_bias.astype(F32), pad).reshape(1, LANES)
    return w_in[:, :main], w_gate, alog_row, dtb_row


def _gdn_layer_prompt(x, gain, w_in, conv_w, a_log, dt_bias, o_gain, w_out, *, batch, seq):
    conv_dim = GDN_HEADS * (2 * GDN_DK + GDN_DV)
    w_main, w_gate, alog_row, dtb_row = _gdn_weights(w_in, a_log, dt_bias)
    proj = _matmul(x, w_main, gain=gain, tm=512, tn=2048, name="gdn_in_proj")
    gates = _matmul(x, w_gate, gain=gain, hp=True, tm=512, name="gdn_gate_proj")
    o, s_fin = _gdn_prompt(proj, gates, conv_w, alog_row, dtb_row, o_gain, batch=batch, seq=seq)
    y = _matmul(o, w_out, res=x, tm=512, name="gdn_out_proj")
    conv_state = proj.reshape(batch, seq, -1)[:, seq - (CONV_W - 1):, :conv_dim]
    return y, conv_state, s_fin


def _gdn_layer_step(x, gain, conv_prev, s_prev, w_in, conv_w, a_log, dt_bias, o_gain, w_out):
    w_main, w_gate, alog_row, dtb_row = _gdn_weights(w_in, a_log, dt_bias)
    proj = _matmul(x, w_main, gain=gain, hp=True, tn=1024, name="gdn_in_proj_s")
    gates = _matmul(x, w_gate, gain=gain, hp=True, name="gdn_gate_proj_s")
    o, conv_new, s_new = _gdn_step(proj, gates, conv_prev, s_prev, conv_w, alog_row, dtb_row, o_gain)
    y = _matmul(o.reshape(x.shape[0], -1), w_out, res=x, hp=True, name="gdn_out_proj_s")
    return y, conv_new, s_new


def _moba_layer_prompt(x, gain, w_in, w_out, *, batch, seq):
    proj = _matmul(x, w_in, gain=gain, tm=512, name="att_in_proj")
    cos_t, sin_a, sin_b = _rope_tables(jnp.arange(seq, dtype=jnp.int32))
    o, k_rot = _moba_prompt(proj, cos_t, sin_a, sin_b, batch=batch, seq=seq)
    y = _matmul(o, w_out, res=x, tm=512, name="att_out_proj")
    return y, k_rot, proj[:, (ATT_HEADS + KV_HEADS) * HEAD_DIM:]


def _moba_layer_step(x, gain, cache_k, cache_v, page_table, w_in, w_out):
    b = x.shape[0]
    hd = HEAD_DIM
    n_pool, page = cache_k.shape[0], cache_k.shape[1]
    past_len = page_table.shape[1] * page
    own = past_len // MOBA_BLOCK
    assert own * MOBA_BLOCK == past_len and own >= 1
    n_sel = min(MOBA_TOPK, own)
    ppb = MOBA_BLOCK // page
    proj = _matmul(x, w_in, gain=gain, hp=True, tn=1024, name="att_in_proj_s")
    ck = cache_k.reshape(n_pool, page * KV_HEADS, hd)
    cv = cache_v.reshape(n_pool, page * KV_HEADS, hd)
    kmeans = _cache_block_means(ck, page_table)
    cos_t, sin_a, sin_b = _rope_tables(jnp.full((1,), past_len, jnp.int32))
    q, k_rot, sel = _moba_select(proj, kmeans, cos_t, sin_a, sin_b, n_sel=n_sel)
    logical = sel[:, :, :n_sel, None] * ppb + jnp.arange(ppb, dtype=jnp.int32)
    pages = jnp.take_along_axis(page_table, logical.reshape(b, -1), axis=1).reshape(-1)
    v_new = proj[:, (ATT_HEADS + KV_HEADS) * hd:]
    o = _moba_step(q, k_rot.reshape(b, KV_HEADS, hd), v_new.reshape(b, KV_HEADS, hd), ck, cv, pages)
    y = _matmul(o, w_out, res=x, hp=True, name="att_out_proj_s")
    return y, k_rot.reshape(b, KV_HEADS * hd), v_new


def _norm_kernel(x_ref, g_ref, o_ref):
    o_ref[...] = _rms(x_ref[...], g_ref[...])


def _final_norm(x, gain, *, tm=512):
    m, d = x.shape
    tm = min(tm, m)
    return pl.pallas_call(
        _norm_kernel,
        out_shape=jax.ShapeDtypeStruct((m, d), F32),
        grid=(m // tm,),
        in_specs=[pl.BlockSpec((tm, d), lambda i: (i, 0)), pl.BlockSpec((1, d), lambda i: (0, 0))],
        out_specs=pl.BlockSpec((tm, d), lambda i: (i, 0)),
        compiler_params=_params("parallel"),
        name="final_norm",
    )(x, gain.reshape(1, d))


def kernel(x_prompt, x_sample, state_conv, state_delta, cache_k, cache_v, page_table, norm_mix, norm_ffn,
           norm_final, gdn_w_in, gdn_conv_w, gdn_a_log, gdn_dt_bias, gdn_o_norm, gdn_w_out, att_w_in,
           att_w_out, ffn_w_gate, ffn_w_up, ffn_w_down, moe_router, moe_w_gate, moe_w_up, moe_w_down):
    bp, seq, d = x_prompt.shape
    db, dec_seq, _ = x_sample.shape
    assert dec_seq == 1
    depth = norm_mix.shape[0]
    hp = x_prompt.reshape(bp * seq, d)
    hs = x_sample.reshape(db, d)
    conv_p, delta_p, k_p, v_p = [], [], [], []
    conv_s, delta_s, k_s, v_s = [], [], [], []
    for i in range(depth):
        j = i // 2
        last = i == depth - 1
        if i % 2 == 0:
            gdn = (gdn_w_in[j], gdn_conv_w[j], gdn_a_log[j], gdn_dt_bias[j], gdn_o_norm[j], gdn_w_out[j])
            hp, cp, sp = _gdn_layer_prompt(hp, norm_mix[i], *gdn, batch=bp, seq=seq)
            hs, cs, ss = _gdn_layer_step(hs, norm_mix[i], state_conv[j], state_delta[j], *gdn)
            conv_p.append(cp)
            delta_p.append(sp.astype(state_delta.dtype))
            conv_s.append(cs)
            delta_s.append(ss.astype(state_delta.dtype))
            ffn = (norm_ffn[i], ffn_w_gate[j], ffn_w_up[j], ffn_w_down[j])
            hp = _ffn(hp, *ffn)
            hs = _ffn(hs, *ffn, hp=True)
            if last:
                hp = _final_norm(hp, norm_final)
                hs = _final_norm(hs, norm_final)
        else:
            hp, kp, vp = _moba_layer_prompt(hp, norm_mix[i], att_w_in[j], att_w_out[j], batch=bp, seq=seq)
            hs, kn, vn = _moba_layer_step(hs, norm_mix[i], cache_k[j], cache_v[j], page_table,
                                          att_w_in[j], att_w_out[j])
            k_p.append(kp.reshape(bp, seq, KV_HEADS, HEAD_DIM))
            v_p.append(vp.reshape(bp, seq, KV_HEADS, HEAD_DIM))
            k_s.append(kn.reshape(db, 1, KV_HEADS, HEAD_DIM))
            v_s.append(vn.reshape(db, 1, KV_HEADS, HEAD_DIM))
            moe = (norm_ffn[i], moe_router[j], moe_w_gate[j], moe_w_up[j], moe_w_down[j])
            fg = norm_final if last else None
            hp = _moe_layer(hp, *moe, fg, tm=512)
            hs = _moe_layer(hs, *moe, fg, tm=32, hp=True)
    return (hp.reshape(bp, seq, d), hs.reshape(db, 1, d),
            jnp.stack(conv_p), jnp.stack(delta_p), jnp.stack(k_p), jnp.stack(v_p),
            jnp.stack(conv_s), jnp.stack(delta_s), jnp.stack(k_s), jnp.stack(v_s))
```

```python
import functools
import math

import jax
import jax.numpy as jnp
from jax import lax
from jax.experimental import pallas as pl
from jax.experimental.pallas import tpu as pltpu

F32 = jnp.float32
BF16 = jnp.bfloat16

GDN_HEADS = 8
GDN_DK = 128
GDN_DV = 128
CONV_W = 4
GDN_CHUNK = 64
ATT_HEADS = 8
KV_HEADS = 4
HEAD_DIM = 128
ROT_DIM = HEAD_DIM // 4
ROPE_THETA = 500000.0
MOBA_BLOCK = 256
MOBA_TOPK = 3
N_EXPERTS = 8
EPS = 1e-6

LANES = 128
SUBLANES = 8
VMEM_LIMIT = 56 * 1024 * 1024

NEG_INF = float("-inf")


def _params(*sem):
    return pltpu.CompilerParams(dimension_semantics=sem, vmem_limit_bytes=VMEM_LIMIT)


def _rms(x, gain):
    return x * lax.rsqrt(jnp.mean(x * x, axis=-1, keepdims=True) + EPS) * gain


def _silu(x):
    return x * jax.nn.sigmoid(x)


def _softplus(x):
    return jnp.maximum(x, 0.0) + jnp.log1p(jnp.exp(-jnp.abs(x)))


def _split2(a):
    hi = a.astype(BF16)
    lo = (a - hi.astype(F32)).astype(BF16)
    return hi, lo


def _split3(a):
    hi = a.astype(BF16)
    r = a - hi.astype(F32)
    mid = r.astype(BF16)
    lo = (r - mid.astype(F32)).astype(BF16)
    return hi, mid, lo


_NN = (((1,), (0,)), ((), ()))
_NT = (((1,), (1,)), ((), ()))
_TN = (((0,), (0,)), ((), ()))


def _dot(a, b, dims=_NN):
    return lax.dot_general(a, b, dims, preferred_element_type=F32)


def _dot_hp(a, b, dims=_NN):
    ah, al = _split2(a)
    bh, bl = _split2(b)
    return _dot(ah, bh, dims) + (_dot(al, bh, dims) + _dot(ah, bl, dims))


def _mm_kernel(*refs, has_gain, has_res, hp):
    it = iter(refs)
    a_ref = next(it)
    g_ref = next(it) if has_gain else None
    w_ref = next(it)
    r_ref = next(it) if has_res else None
    o_ref = next(it)
    an_ref = next(it)

    @pl.when(pl.program_id(1) == 0)
    def _():
        a = a_ref[...].astype(F32)
        if has_gain:
            a = _rms(a, g_ref[...])
        an_ref[...] = a.astype(an_ref.dtype)

    acc = (_dot_hp if hp else _dot)(an_ref[...], w_ref[...])
    if has_res:
        acc = acc + r_ref[...]
    o_ref[...] = acc.astype(o_ref.dtype)


def _matmul(a, w, *, gain=None, res=None, hp=False, tm=512, tn=None, out_dtype=F32, name="matmul"):
    m, k = a.shape
    n = w.shape[1]
    tm = min(tm, m)
    tn = n if tn is None else min(tn, n)
    assert m % tm == 0 and n % tn == 0
    args = [a]
    in_specs = [pl.BlockSpec((tm, k), lambda i, j: (i, 0))]
    if gain is not None:
        args.append(gain.reshape(1, k).astype(F32))
        in_specs.append(pl.BlockSpec((1, k), lambda i, j: (0, 0)))
    args.append(w if hp else w.astype(BF16))
    in_specs.append(pl.BlockSpec((k, tn), lambda i, j: (0, j)))
    if res is not None:
        args.append(res)
        in_specs.append(pl.BlockSpec((tm, tn), lambda i, j: (i, j)))
    scratch = [pltpu.VMEM((tm, k), F32 if hp else BF16)]
    return pl.pallas_call(
        functools.partial(_mm_kernel, has_gain=gain is not None, has_res=res is not None, hp=hp),
        out_shape=jax.ShapeDtypeStruct((m, n), out_dtype),
        grid=(m // tm, n // tn),
        in_specs=in_specs,
        out_specs=pl.BlockSpec((tm, tn), lambda i, j: (i, j)),
        scratch_shapes=scratch,
        compiler_params=_params("parallel", "arbitrary"),
        name=name,
    )(*args)


def _swiglu_step(xn, wg, wu, wd, hp):
    dot = _dot_hp if hp else _dot
    act = _silu(dot(xn, wg)) * dot(xn, wu)
    return dot(act if hp else act.astype(BF16), wd)


def _ffn_kernel(x_ref, g_ref, wg_ref, wu_ref, wd_ref, o_ref, xn_ref, acc_ref, *, hp):
    f = pl.program_id(1)

    @pl.when(f == 0)
    def _():
        xn_ref[...] = _rms(x_ref[...], g_ref[...]).astype(xn_ref.dtype)
        acc_ref[...] = jnp.zeros_like(acc_ref)

    acc_ref[...] += _swiglu_step(xn_ref[...], wg_ref[...], wu_ref[...], wd_ref[...], hp)

    @pl.when(f == pl.num_programs(1) - 1)
    def _():
        o_ref[...] = x_ref[...] + acc_ref[...]


def _ffn(x, gain, wg, wu, wd, *, hp=False, tm=1024, tf=512):
    m, d = x.shape
    dff = wg.shape[1]
    tm = min(tm, m)
    assert m % tm == 0 and dff % tf == 0
    wdt = F32 if hp else BF16
    return pl.pallas_call(
        functools.partial(_ffn_kernel, hp=hp),
        out_shape=jax.ShapeDtypeStruct((m, d), F32),
        grid=(m // tm, dff // tf),
        in_specs=[
            pl.BlockSpec((tm, d), lambda i, f: (i, 0)),
            pl.BlockSpec((1, d), lambda i, f: (0, 0)),
            pl.BlockSpec((d, tf), lambda i, f: (0, f)),
            pl.BlockSpec((d, tf), lambda i, f: (0, f)),
            pl.BlockSpec((tf, d), lambda i, f: (f, 0)),
        ],
        out_specs=pl.BlockSpec((tm, d), lambda i, f: (i, 0)),
        scratch_shapes=[pltpu.VMEM((tm, d), wdt), pltpu.VMEM((tm, d), F32)],
        compiler_params=_params("parallel", "arbitrary"),
        name="ffn",
    )(x, gain.reshape(1, d), wg.astype(wdt), wu.astype(wdt), wd.astype(wdt))


def _bdot(a, b, dims=_NN):
    return _dot(a.astype(BF16), b.astype(BF16), dims)


def _neumann_inverse_minus_eye(mats):
    c = mats[0].shape[0]
    ps = [-a for a in mats]
    ns = list(ps)
    for _ in range(int(math.log2(c)) - 1):
        ps = [_bdot(p, p) for p in ps]
        ns = [n + p + _bdot(n, p) for n, p in zip(ns, ps)]
    return ns


def _gdn_kernel(proj_ref, gate_ref, convw_ref, alog_ref, dtb_ref, ogain_ref,
                o_ref, sfin_ref, s_ref, tail_ref, *, tt, c):
    h_n, dk, dv = GDN_HEADS, GDN_DK, GDN_DV
    t = pl.program_id(1)

    @pl.when(t == 0)
    def _():
        s_ref[...] = jnp.zeros_like(s_ref)
        tail_ref[...] = jnp.zeros_like(tail_ref)

    gt = gate_ref[...]
    beta_all = jax.nn.sigmoid(gt)
    g_all = -jnp.exp(alog_ref[...]) * _softplus(gt + dtb_ref[...])

    ri = lax.broadcasted_iota(jnp.int32, (c, c), 0)
    ci = lax.broadcasted_iota(jnp.int32, (c, c), 1)
    tri = ri >= ci
    strict = ri > ci
    tri_b = tri.astype(BF16)
    ri2 = lax.broadcasted_iota(jnp.int32, (c, LANES), 0)
    ci2 = lax.broadcasted_iota(jnp.int32, (c, LANES), 1)
    upper_b = (ri2 <= ci2).astype(BF16)

    n_chunks = tt // c
    gams, gam_ts = [], []
    for ck in range(n_chunks):
        parts = _split3(g_all[ck * c:(ck + 1) * c, :])
        gam = sum(_dot(tri_b, p) for p in parts)
        gam_t = sum(_dot(p, upper_b, _TN) for p in parts)
        gams.append(gam)
        gam_ts.append(gam_t)

    def conv_silu(off):
        raw = proj_ref[:, off:off + LANES]
        xs = jnp.concatenate([tail_ref[:, off:off + LANES], raw], axis=0)
        w = convw_ref[:, off:off + LANES]
        acc = raw * w[CONV_W - 1:CONV_W]
        for j in range(CONV_W - 1):
            lo = SUBLANES - (CONV_W - 1) + j
            acc = acc + xs[lo:lo + tt] * w[j:j + 1]
        return _silu(acc)

    def l2n(x):
        return x * lax.rsqrt(jnp.sum(x * x, axis=-1, keepdims=True) + 1e-6)

    heads = range(h_n)
    pairs = [(ck, h) for ck in range(n_chunks) for h in heads]
    z_off = 2 * h_n * dk + h_n * dv
    q_all = [l2n(conv_silu(h * dk)) * (dk ** -0.5) for h in heads]
    k_all = [l2n(conv_silu(h_n * dk + h * dk)) for h in heads]
    v_all = [conv_silu(2 * h_n * dk + h * dv) for h in heads]

    def rows(x, ck):
        return x[ck * c:(ck + 1) * c]

    qc = [rows(q_all[h], ck) for ck, h in pairs]
    kc = [rows(k_all[h], ck) for ck, h in pairs]
    bcol = [rows(beta_all, ck)[:, h:h + 1] for ck, h in pairs]
    col = [gams[ck][:, h_n + h:h_n + h + 1] for ck, h in pairs]
    dm = [jnp.exp(jnp.where(tri, cl - gam_ts[ck][h_n + h:h_n + h + 1, :c], NEG_INF))
          for cl, (ck, h) in zip(col, pairs)]
    kb = [x * b for x, b in zip(kc, bcol)]
    kq = [_bdot(jnp.concatenate([x, y], axis=0), z, _NT) for x, y, z in zip(kb, qc, kc)]
    a = [jnp.where(strict, x[:c] * d, 0.0) for x, d in zip(kq, dm)]
    attn = [(x[c:] * d).astype(BF16) for x, d in zip(kq, dm)]
    ninv = _neumann_inverse_minus_eye(a)
    eg = [jnp.exp(cl) for cl in col]
    rhs = [jnp.concatenate([rows(v_all[h], ck) * b, x * e], axis=1)
           for (ck, h), b, x, e in zip(pairs, bcol, kb, eg)]
    sol = [r + _bdot(n, r) for n, r in zip(ninv, rhs)]
    glast = [cl[c - 1:c, :] for cl in col]
    lhs = [jnp.concatenate([x[:, dv:], y * e], axis=0).astype(BF16)
           for x, y, e in zip(sol, qc, eg)]
    k_dec = [(x * jnp.exp(g - cl)).astype(BF16) for x, g, cl in zip(kc, glast, col)]
    g_end = [jnp.exp(g) for g in glast]

    ogain = ogain_ref[...]
    for ck in range(n_chunks):
        idx = [ck * h_n + h for h in heads]
        s_old = [s_ref[h] for h in heads]
        both = [_dot(lhs[i], s.astype(BF16)) for i, s in zip(idx, s_old)]
        ub = [(sol[i][:, :dv] - b[:c]).astype(BF16) for i, b in zip(idx, both)]
        o = [b[c:] + _dot(attn[i], u) for i, b, u in zip(idx, both, ub)]
        for h, i, s, u in zip(heads, idx, s_old, ub):
            s_ref[h] = s * g_end[i] + _dot(k_dec[i], u, _TN)
        for h, x in zip(heads, o):
            zc = proj_ref[ck * c:(ck + 1) * c, z_off + h * dv:z_off + (h + 1) * dv]
            o_ref[ck * c:(ck + 1) * c, h * dv:(h + 1) * dv] = (_rms(x, ogain) * _silu(zc)).astype(o_ref.dtype)

    tail_ref[...] = proj_ref[tt - SUBLANES:tt, :tail_ref.shape[1]]

    @pl.when(t == pl.num_programs(1) - 1)
    def _():
        sfin_ref[0] = s_ref[...]


def _gdn_prompt(proj, gates, conv_w, alog_row, dtb_row, o_gain, *, batch, seq, tt=128):
    h_n, dk, dv = GDN_HEADS, GDN_DK, GDN_DV
    conv_dim = h_n * (2 * dk + dv)
    width = proj.shape[1]
    tt = min(tt, seq)
    c = math.gcd(seq, GDN_CHUNK)
    assert seq % tt == 0 and tt % c == 0 and tt >= SUBLANES
    nt = seq // tt
    return pl.pallas_call(
        functools.partial(_gdn_kernel, tt=tt, c=c),
        out_shape=(jax.ShapeDtypeStruct((batch * seq, h_n * dv), BF16),
                   jax.ShapeDtypeStruct((batch, h_n, dk, dv), F32)),
        grid=(batch, nt),
        in_specs=[
            pl.BlockSpec((tt, width), lambda b, t: (b * nt + t, 0)),
            pl.BlockSpec((tt, LANES), lambda b, t: (b * nt + t, 0)),
            pl.BlockSpec((CONV_W, conv_dim), lambda b, t: (0, 0)),
            pl.BlockSpec((1, LANES), lambda b, t: (0, 0)),
            pl.BlockSpec((1, LANES), lambda b, t: (0, 0)),
            pl.BlockSpec((1, dv), lambda b, t: (0, 0)),
        ],
        out_specs=(pl.BlockSpec((tt, h_n * dv), lambda b, t: (b * nt + t, 0)),
                   pl.BlockSpec((1, h_n, dk, dv), lambda b, t: (b, 0, 0, 0))),
        scratch_shapes=[pltpu.VMEM((h_n, dk, dv), F32), pltpu.VMEM((SUBLANES, conv_dim), F32)],
        compiler_params=_params("parallel", "arbitrary"),
        name="gdn_prompt",
    )(proj, gates, conv_w, alog_row, dtb_row, o_gain.reshape(1, dv))


def _gdn_step_kernel(proj_ref, gate_ref, cprev_ref, s_ref, convw_ref, alog_ref, dtb_ref, ogain_ref,
                     o_ref, cnew_ref, snew_ref):
    h_n, dk, dv = GDN_HEADS, GDN_DK, GDN_DV
    conv_dim = h_n * (2 * dk + dv)
    u_row = proj_ref[0]
    prev = cprev_ref[0]
    w = convw_ref[...]
    raw = u_row[:, :conv_dim]
    acc = raw * w[CONV_W - 1:CONV_W]
    for j in range(CONV_W - 1):
        acc = acc + prev[j:j + 1] * w[j:j + 1]
    qkv = _silu(acc)
    for j in range(CONV_W - 2):
        cnew_ref[0, j:j + 1, :] = prev[j + 1:j + 2]
    cnew_ref[0, CONV_W - 2:CONV_W - 1, :] = raw

    gt = gate_ref[0]
    beta_all = jax.nn.sigmoid(gt)
    g_all = -jnp.exp(alog_ref[...]) * _softplus(gt + dtb_ref[...])
    rows = lax.broadcasted_iota(jnp.int32, (SUBLANES, LANES), 0)
    ogain = ogain_ref[...]

    def l2n(x):
        return x * lax.rsqrt(jnp.sum(x * x, axis=-1, keepdims=True) + 1e-6)

    def as_row0(x, y=None):
        out = jnp.where(rows == 0, jnp.broadcast_to(x, (SUBLANES, LANES)), 0.0)
        if y is not None:
            out = jnp.where(rows == 1, jnp.broadcast_to(y, (SUBLANES, LANES)), out)
        return out

    for h in range(h_n):
        q = l2n(qkv[:, h * dk:(h + 1) * dk]) * (dk ** -0.5)
        k = l2n(qkv[:, h_n * dk + h * dk:h_n * dk + (h + 1) * dk])
        v = qkv[:, 2 * h_n * dk + h * dv:2 * h_n * dk + (h + 1) * dv]
        z = u_row[:, conv_dim + h * dv:conv_dim + (h + 1) * dv]
        beta = beta_all[:, h:h + 1]
        eg = jnp.exp(g_all[:, h_n + h:h_n + h + 1])
        s = s_ref[0, h]
        kcd = k * (beta * eg)
        q_dec = q * eg
        both = _dot_hp(as_row0(kcd, q_dec), s)
        u = v * beta - both[0:1]
        o = both[1:2] + jnp.sum(q * k, axis=-1, keepdims=True) * u
        snew_ref[0, h] = s * eg + _dot_hp(as_row0(k), as_row0(u), _TN)
        o_ref[0, :, h * dv:(h + 1) * dv] = (_rms(o, ogain) * _silu(z)).astype(o_ref.dtype)


def _gdn_step(proj, gates, conv_prev, s_prev, conv_w, alog_row, dtb_row, o_gain):
    h_n, dk, dv = GDN_HEADS, GDN_DK, GDN_DV
    conv_dim = h_n * (2 * dk + dv)
    b, width = proj.shape
    return pl.pallas_call(
        _gdn_step_kernel,
        out_shape=(jax.ShapeDtypeStruct((b, 1, h_n * dv), F32),
                   jax.ShapeDtypeStruct((b, CONV_W - 1, conv_dim), F32),
                   jax.ShapeDtypeStruct((b, h_n, dk, dv), F32)),
        grid=(b,),
        in_specs=[
            pl.BlockSpec((1, 1, width), lambda i: (i, 0, 0)),
            pl.BlockSpec((1, 1, LANES), lambda i: (i, 0, 0)),
            pl.BlockSpec((1, CONV_W - 1, conv_dim), lambda i: (i, 0, 0)),
            pl.BlockSpec((1, h_n, dk, dv), lambda i: (i, 0, 0, 0)),
            pl.BlockSpec((CONV_W, conv_dim), lambda i: (0, 0)),
            pl.BlockSpec((1, LANES), lambda i: (0, 0)),
            pl.BlockSpec((1, LANES), lambda i: (0, 0)),
            pl.BlockSpec((1, dv), lambda i: (0, 0)),
        ],
        out_specs=(pl.BlockSpec((1, 1, h_n * dv), lambda i: (i, 0, 0)),
                   pl.BlockSpec((1, CONV_W - 1, conv_dim), lambda i: (i, 0, 0)),
                   pl.BlockSpec((1, h_n, dk, dv), lambda i: (i, 0, 0, 0))),
        compiler_params=_params("parallel"),
        name="gdn_step",
    )(proj.reshape(b, 1, width), gates.reshape(b, 1, LANES), conv_prev, s_prev, conv_w,
      alog_row, dtb_row, o_gain.reshape(1, dv))


def _rope(x, cos, sin_a, sin_b):
    half = ROT_DIM // 2
    return (x * cos + pltpu.roll(x, HEAD_DIM - half, 1) * sin_a + pltpu.roll(x, half, 1) * sin_b)


def _rope_tables(pos):
    half = ROT_DIM // 2
    inv = jnp.power(ROPE_THETA, -jnp.arange(half, dtype=F32) * (2.0 / ROT_DIM))
    ang = pos.astype(F32)[:, None] * inv[None, :]
    cos, sin = jnp.cos(ang), jnp.sin(ang)
    n = pos.shape[0]
    rest = HEAD_DIM - ROT_DIM
    cos_t = jnp.concatenate([cos, cos, jnp.ones((n, rest), F32)], axis=1)
    sin_a = jnp.concatenate([-sin, jnp.zeros((n, half + rest), F32)], axis=1)
    sin_b = jnp.concatenate([jnp.zeros((n, half), F32), sin, jnp.zeros((n, rest), F32)], axis=1)
    return cos_t, sin_a, sin_b


def _moba_kernel(q_ref, k_ref, v_ref, cos_ref, sa_ref, sb_ref, o_ref, kout_ref, kb_ref, vb_ref, *, seq):
    blk, hd = MOBA_BLOCK, HEAD_DIM
    group = ATT_HEADS // KV_HEADS
    nb = seq // blk
    nq = group * blk
    scale = hd ** -0.5

    kout_ref[...] = _rope(k_ref[...], cos_ref[...], sa_ref[...], sb_ref[...])
    kb_ref[...] = kout_ref[...].astype(BF16)
    vb_ref[...] = v_ref[...].astype(BF16)
    kmean = jnp.concatenate(
        [jnp.sum(kout_ref[n * blk:(n + 1) * blk, :], axis=0, keepdims=True) for n in range(nb)],
        axis=0) * (1.0 / blk)

    blk_id = lax.broadcasted_iota(jnp.int32, (nb, nq), 0)
    key_r = lax.broadcasted_iota(jnp.int32, (blk, nq), 0)
    qry_c = lax.broadcasted_iota(jnp.int32, (blk, nq), 1) % blk
    causal = key_r <= qry_c

    for i in range(nb):
        rows = slice(i * blk, (i + 1) * blk)
        cos, sa, sb = cos_ref[rows, :], sa_ref[rows, :], sb_ref[rows, :]
        qs = [_rope(q_ref[rows, g * hd:(g + 1) * hd], cos, sa, sb) for g in range(group)]
        q = jnp.concatenate(qs, axis=0)
        qb = q.astype(BF16)

        if i > MOBA_TOPK:
            gate = jnp.where(blk_id < i, _dot_hp(kmean, q, _NT), NEG_INF)
            rank = jnp.zeros((nb, nq), F32)
            for m in range(i):
                gm = gate[m:m + 1, :]
                rank = rank + jnp.where((gm > gate) | ((gm == gate) & (m < blk_id)), 1.0, 0.0)
            sel = jnp.where((blk_id < i) & (rank < MOBA_TOPK), 1.0, 0.0)
        else:
            sel = None

        ss = []
        for j in range(i + 1):
            s = _dot(kb_ref[j * blk:(j + 1) * blk, :], qb, _NT) * scale
            if j == i:
                s = jnp.where(causal, s, NEG_INF)
            elif sel is not None:
                s = jnp.where(sel[j:j + 1, :] > 0.0, s, NEG_INF)
            ss.append(s)
        m_all = functools.reduce(jnp.maximum, [jnp.max(s, axis=0, keepdims=True) for s in ss])
        ps = [jnp.exp(s - m_all) for s in ss]
        l_all = functools.reduce(jnp.add, [jnp.sum(p, axis=0, keepdims=True) for p in ps])
        acc = functools.reduce(
            jnp.add, [_dot(vb_ref[j * blk:(j + 1) * blk, :], p.astype(BF16), _TN) for j, p in enumerate(ps)])
        out = (acc / l_all).T
        for g in range(group):
            o_ref[rows, g * hd:(g + 1) * hd] = out[g * blk:(g + 1) * blk].astype(o_ref.dtype)


def _moba_prompt(proj, cos_t, sin_a, sin_b, *, batch, seq):
    hd = HEAD_DIM
    group = ATT_HEADS // KV_HEADS
    assert seq % MOBA_BLOCK == 0
    nb = seq // MOBA_BLOCK
    tbl = pl.BlockSpec((seq, hd), lambda b, g: (0, 0))
    return pl.pallas_call(
        functools.partial(_moba_kernel, seq=seq),
        out_shape=(jax.ShapeDtypeStruct((batch * seq, ATT_HEADS * hd), BF16),
                   jax.ShapeDtypeStruct((batch * seq, KV_HEADS * hd), F32)),
        grid=(batch, KV_HEADS),
        in_specs=[
            pl.BlockSpec((seq, group * hd), lambda b, g: (b, g)),
            pl.BlockSpec((seq, hd), lambda b, g: (b, ATT_HEADS + g)),
            pl.BlockSpec((seq, hd), lambda b, g: (b, ATT_HEADS + KV_HEADS + g)),
            tbl, tbl, tbl,
        ],
        out_specs=(pl.BlockSpec((seq, group * hd), lambda b, g: (b, g)),
                   pl.BlockSpec((seq, hd), lambda b, g: (b, g))),
        scratch_shapes=[pltpu.VMEM((seq, hd), BF16), pltpu.VMEM((seq, hd), BF16)],
        compiler_params=_params("parallel", "parallel"),
        name="moba_prompt",
    )(proj, proj, proj, cos_t, sin_a, sin_b)


def _kmean_kernel(pt_ref, *refs, pages, pages_per_block):
    page_refs, o_ref = refs[:pages], refs[pages]
    rows = page_refs[0].shape[1]
    tokens = pages_per_block * rows // KV_HEADS
    for n in range(pages // pages_per_block):
        part = jnp.zeros((SUBLANES, HEAD_DIM), F32)
        for p in range(pages_per_block):
            x = page_refs[n * pages_per_block + p][0]
            part = part + jnp.sum(x.reshape(rows // SUBLANES, SUBLANES, HEAD_DIM), axis=0)
        acc = part[:KV_HEADS]
        for r in range(1, SUBLANES // KV_HEADS):
            acc = acc + part[r * KV_HEADS:(r + 1) * KV_HEADS]
        o_ref[0, n] = acc * (1.0 / tokens)


def _cache_block_means(cache, page_table, *, blocks_per_step=8):
    n_pool, rows, hd = cache.shape
    page = rows // KV_HEADS
    b, n_pages = page_table.shape
    ppb = MOBA_BLOCK // page
    assert ppb * page == MOBA_BLOCK and n_pages % ppb == 0 and rows % SUBLANES == 0
    n_blocks = n_pages // ppb
    bps = math.gcd(blocks_per_step, n_blocks)
    pages = bps * ppb
    specs = [pl.BlockSpec((1, rows, hd), lambda i, s, pt, p=p: (pt[i, s * pages + p], 0, 0))
             for p in range(pages)]
    return pl.pallas_call(
        functools.partial(_kmean_kernel, pages=pages, pages_per_block=ppb),
        out_shape=jax.ShapeDtypeStruct((b, n_blocks, KV_HEADS, hd), F32),
        grid_spec=pltpu.PrefetchScalarGridSpec(
            num_scalar_prefetch=1,
            grid=(b, n_blocks // bps),
            in_specs=specs,
            out_specs=pl.BlockSpec((1, bps, KV_HEADS, hd), lambda i, s, pt: (i, s, 0, 0)),
        ),
        compiler_params=_params("parallel", "arbitrary"),
        name="cache_block_means",
    )(page_table, *([cache] * pages))


def _moba_select_kernel(proj_ref, km_ref, cos_ref, sa_ref, sb_ref, q_ref, k_ref, sel_ref, *, n_sel):
    hd = HEAD_DIM
    group = ATT_HEADS // KV_HEADS
    row = proj_ref[0]
    cos, sa, sb = cos_ref[...], sa_ref[...], sb_ref[...]
    rows = lax.broadcasted_iota(jnp.int32, (SUBLANES, hd), 0)
    q = jnp.zeros((SUBLANES, hd), F32)
    for h in range(ATT_HEADS):
        qh = _rope(row[:, h * hd:(h + 1) * hd], cos, sa, sb)
        q = jnp.where(rows == h, jnp.broadcast_to(qh, (SUBLANES, hd)), q)
    q_ref[0] = q
    for g in range(KV_HEADS):
        off = (ATT_HEADS + g) * hd
        k_ref[0, :, g * hd:(g + 1) * hd] = _rope(row[:, off:off + hd], cos, sa, sb)

    nb = km_ref.shape[1]
    hrow = lax.broadcasted_iota(jnp.int32, (ATT_HEADS, nb), 0)
    lane = lax.broadcasted_iota(jnp.int32, (ATT_HEADS, nb), 1)
    gate = jnp.zeros((ATT_HEADS, nb), F32)
    for g in range(KV_HEADS):
        gg = _dot_hp(q, km_ref[0, :, g, :], _NT)
        gate = jnp.where(hrow // group == g, gg, gate)
    out_lane = lax.broadcasted_iota(jnp.int32, (ATT_HEADS, LANES), 1)
    sel = jnp.zeros((ATT_HEADS, LANES), jnp.int32)
    for t in range(n_sel):
        m = jnp.max(gate, axis=-1, keepdims=True)
        idx = jnp.min(jnp.where(gate == m, lane, nb), axis=-1, keepdims=True)
        sel = jnp.where(out_lane == t, idx, sel)
        gate = jnp.where(lane == idx, NEG_INF, gate)
    sel_ref[0] = sel


def _moba_select(proj, kmeans, cos_t, sin_a, sin_b, *, n_sel):
    b, width = proj.shape
    nb = kmeans.shape[1]
    hd = HEAD_DIM
    tbl = pl.BlockSpec((1, hd), lambda i: (0, 0))
    return pl.pallas_call(
        functools.partial(_moba_select_kernel, n_sel=n_sel),
        out_shape=(jax.ShapeDtypeStruct((b, ATT_HEADS, hd), F32),
                   jax.ShapeDtypeStruct((b, 1, KV_HEADS * hd), F32),
                   jax.ShapeDtypeStruct((b, ATT_HEADS, LANES), jnp.int32)),
        grid=(b,),
        in_specs=[pl.BlockSpec((1, 1, width), lambda i: (i, 0, 0)),
                  pl.BlockSpec((1, nb, KV_HEADS, hd), lambda i: (i, 0, 0, 0)),
                  tbl, tbl, tbl],
        out_specs=(pl.BlockSpec((1, ATT_HEADS, hd), lambda i: (i, 0, 0)),
                   pl.BlockSpec((1, 1, KV_HEADS * hd), lambda i: (i, 0, 0)),
                   pl.BlockSpec((1, ATT_HEADS, LANES), lambda i: (i, 0, 0))),
        compiler_params=_params("parallel"),
        name="moba_select",
    )(proj.reshape(b, 1, width), kmeans, cos_t, sin_a, sin_b)


def _moba_step_kernel(pg_ref, q_ref, kn_ref, vn_ref, *refs, n_pages):
    k_refs, v_refs, o_ref = refs[:n_pages], refs[n_pages:2 * n_pages], refs[2 * n_pages]
    hd = HEAD_DIM
    group = ATT_HEADS // KV_HEADS
    scale = hd ** -0.5
    h = pl.program_id(1)
    g = h // group
    rows = k_refs[0].shape[1]
    q = q_ref[0, pl.ds(h, 1), :]
    q8 = jnp.broadcast_to(q, (SUBLANES, hd))
    lane_kvh = lax.broadcasted_iota(jnp.int32, (1, rows), 1) % KV_HEADS
    mine = lane_kvh == g
    kn = kn_ref[0, pl.ds(g, 1), :]
    vn = vn_ref[0, pl.ds(g, 1), :]
    s_self = jnp.sum(q * kn, axis=-1, keepdims=True) * scale
    scores = []
    m = s_self
    for p in range(n_pages):
        s = _dot_hp(q8, k_refs[p][0], _NT)[0:1] * scale
        s = jnp.where(mine, s, NEG_INF)
        scores.append(s)
        m = jnp.maximum(m, jnp.max(s, axis=-1, keepdims=True))
    p_self = jnp.exp(s_self - m)
    l = p_self
    acc = p_self * vn
    for p in range(n_pages):
        e = jnp.exp(scores[p] - m)
        l = l + jnp.sum(e, axis=-1, keepdims=True)
        acc = acc + _dot_hp(jnp.broadcast_to(e, (SUBLANES, rows)), v_refs[p][0])[0:1]
    o_ref[0] = (acc / l).astype(o_ref.dtype)


def _moba_step(q, k_new, v_new, cache_k, cache_v, pages):
    b = q.shape[0]
    hd = HEAD_DIM
    rows = cache_k.shape[1]
    n_pages = pages.shape[0] // (b * ATT_HEADS)

    def page_spec(p):
        return pl.BlockSpec((1, rows, hd), lambda i, h, pg, p=p: (pg[(i * ATT_HEADS + h) * n_pages + p], 0, 0))

    specs = ([pl.BlockSpec((1, ATT_HEADS, hd), lambda i, h, pg: (i, 0, 0)),
              pl.BlockSpec((1, KV_HEADS, hd), lambda i, h, pg: (i, 0, 0)),
              pl.BlockSpec((1, KV_HEADS, hd), lambda i, h, pg: (i, 0, 0))]
             + [page_spec(p) for p in range(n_pages)] * 2)
    out = pl.pallas_call(
        functools.partial(_moba_step_kernel, n_pages=n_pages),
        out_shape=jax.ShapeDtypeStruct((b * ATT_HEADS, 1, hd), F32),
        grid_spec=pltpu.PrefetchScalarGridSpec(
            num_scalar_prefetch=1,
            grid=(b, ATT_HEADS),
            in_specs=specs,
            out_specs=pl.BlockSpec((1, 1, hd), lambda i, h, pg: (i * ATT_HEADS + h, 0, 0)),
        ),
        compiler_params=_params("parallel", "arbitrary"),
        name="moba_step",
    )(pages, q, k_new, v_new, *([cache_k] * n_pages), *([cache_v] * n_pages))
    return out.reshape(b, ATT_HEADS * hd)


def _router_kernel(x_ref, g_ref, w_ref, idx_ref, wts_ref):
    logits = _dot_hp(_rms(x_ref[...], g_ref[...]), w_ref[...])
    lane = lax.broadcasted_iota(jnp.int32, logits.shape, 1)
    logits = jnp.where(lane < N_EXPERTS, logits, NEG_INF)
    m1 = jnp.max(logits, axis=-1, keepdims=True)
    i1 = jnp.min(jnp.where(logits == m1, lane, LANES), axis=-1, keepdims=True)
    rest = jnp.where(lane == i1, NEG_INF, logits)
    m2 = jnp.max(rest, axis=-1, keepdims=True)
    i2 = jnp.min(jnp.where(rest == m2, lane, LANES), axis=-1, keepdims=True)
    e = jnp.exp(m2 - m1)
    w1 = 1.0 / (1.0 + e)
    idx_ref[...] = jnp.where(lane == 0, i1, jnp.where(lane == 1, i2, 0))
    wts_ref[...] = jnp.where(lane == 0, w1, jnp.where(lane == 1, e * w1, 0.0))


def _router(x, gain, w_router, *, tm=512):
    m, d = x.shape
    tm = min(tm, m)
    assert m % tm == 0
    w = jnp.pad(w_router, ((0, 0), (0, LANES - w_router.shape[1])))
    return pl.pallas_call(
        _router_kernel,
        out_shape=(jax.ShapeDtypeStruct((m, LANES), jnp.int32), jax.ShapeDtypeStruct((m, LANES), F32)),
        grid=(m // tm,),
        in_specs=[pl.BlockSpec((tm, d), lambda i: (i, 0)), pl.BlockSpec((1, d), lambda i: (0, 0)),
                  pl.BlockSpec((d, LANES), lambda i: (0, 0))],
        out_specs=(pl.BlockSpec((tm, LANES), lambda i: (i, 0)), pl.BlockSpec((tm, LANES), lambda i: (i, 0))),
        compiler_params=_params("parallel"),
        name="router",
    )(x, gain.reshape(1, d), w)


def _row_copy(src, dst, sem, i, j):
    return pltpu.make_async_copy(src.at[pl.ds(i, 1)], dst.at[pl.ds(j, 1)], sem)


def _gather_kernel(idx_ref, src_ref, out_ref, sem, *, rows):
    base = pl.program_id(0) * rows

    def issue(r, carry):
        _row_copy(src_ref, out_ref, sem, idx_ref[base + r], r).start()
        return carry

    def drain(r, carry):
        _row_copy(src_ref, out_ref, sem, 0, r).wait()
        return carry

    lax.fori_loop(0, rows, issue, 0)
    lax.fori_loop(0, rows, drain, 0)


def _gather_rows(src, idx, *, rows=512):
    n = idx.shape[0]
    rows = math.gcd(rows, n)
    return pl.pallas_call(
        functools.partial(_gather_kernel, rows=rows),
        out_shape=jax.ShapeDtypeStruct((n,) + src.shape[1:], src.dtype),
        grid_spec=pltpu.PrefetchScalarGridSpec(
            num_scalar_prefetch=1,
            grid=(n // rows,),
            in_specs=[pl.BlockSpec(memory_space=pl.ANY)],
            out_specs=pl.BlockSpec((rows,) + src.shape[1:], lambda i, ix: (i,) + (0,) * (src.ndim - 1)),
            scratch_shapes=[pltpu.SemaphoreType.DMA],
        ),
        compiler_params=_params("arbitrary"),
        name="gather_rows",
    )(idx, src)


def _moe_ffn_kernel(te_ref, na_ref, x_ref, g_ref, wg_ref, wu_ref, wd_ref, o_ref, xn_ref, acc_ref, *, hp):
    t, f = pl.program_id(0), pl.program_id(1)
    last = f == pl.num_programs(1) - 1
    active = t < na_ref[0]

    @pl.when(active)
    def _():
        @pl.when(f == 0)
        def _():
            xn_ref[...] = _rms(x_ref[...], g_ref[...]).astype(xn_ref.dtype)
            acc_ref[...] = jnp.zeros_like(acc_ref)

        acc_ref[...] += _swiglu_step(xn_ref[...], wg_ref[0], wu_ref[0], wd_ref[0], hp)

        @pl.when(last)
        def _():
            o_ref[...] = acc_ref[...]

    @pl.when(jnp.logical_and(jnp.logical_not(active), last))
    def _():
        o_ref[...] = jnp.zeros_like(o_ref)


def _moe_ffn(xs, gain, wg, wu, wd, tile_expert, n_active, *, tm, hp=False, tf=512):
    s, d = xs.shape
    dff = wg.shape[2]
    assert s % tm == 0 and dff % tf == 0
    wdt = F32 if hp else BF16
    return pl.pallas_call(
        functools.partial(_moe_ffn_kernel, hp=hp),
        out_shape=jax.ShapeDtypeStruct((s, d), F32),
        grid_spec=pltpu.PrefetchScalarGridSpec(
            num_scalar_prefetch=2,
            grid=(s // tm, dff // tf),
            in_specs=[
                pl.BlockSpec((tm, d), lambda t, f, te, na: (t, 0)),
                pl.BlockSpec((1, d), lambda t, f, te, na: (0, 0)),
                pl.BlockSpec((1, d, tf), lambda t, f, te, na: (te[t], 0, f)),
                pl.BlockSpec((1, d, tf), lambda t, f, te, na: (te[t], 0, f)),
                pl.BlockSpec((1, tf, d), lambda t, f, te, na: (te[t], f, 0)),
            ],
            out_specs=pl.BlockSpec((tm, d), lambda t, f, te, na: (t, 0)),
            scratch_shapes=[pltpu.VMEM((tm, d), wdt), pltpu.VMEM((tm, d), F32)],
        ),
        compiler_params=_params("parallel", "arbitrary"),
        name="moe_ffn",
    )(tile_expert, n_active, xs, gain.reshape(1, d), wg.astype(wdt), wu.astype(wdt), wd.astype(wdt))


def _combine_kernel(slot_ref, x_ref, wts_ref, *refs, tc, has_norm):
    if has_norm:
        gain_ref, y_ref, o_ref, buf_ref, sem = refs
    else:
        y_ref, o_ref, buf_ref, sem = refs
    base = pl.program_id(0) * tc

    def issue(r, carry):
        for k in range(2):
            _row_copy(y_ref, buf_ref.at[k], sem, slot_ref[2 * (base + r) + k], r).start()
        return carry

    def drain(r, carry):
        for k in range(2):
            _row_copy(y_ref, buf_ref.at[k], sem, 0, r).wait()
        return carry

    lax.fori_loop(0, tc, issue, 0)
    lax.fori_loop(0, tc, drain, 0)
    w = wts_ref[...]
    out = x_ref[...] + (w[:, 0:1] * buf_ref[0] + w[:, 1:2] * buf_ref[1])
    if has_norm:
        out = _rms(out, gain_ref[...])
    o_ref[...] = out


def _combine(x, wts, y_sorted, slots, final_gain, *, tc=256):
    m, d = x.shape
    tc = min(tc, m)
    assert m % tc == 0
    has_norm = final_gain is not None
    args = [slots, x, wts]
    in_specs = [pl.BlockSpec((tc, d), lambda i, sl: (i, 0)), pl.BlockSpec((tc, LANES), lambda i, sl: (i, 0))]
    if has_norm:
        args.append(final_gain.reshape(1, d))
        in_specs.append(pl.BlockSpec((1, d), lambda i, sl: (0, 0)))
    args.append(y_sorted)
    in_specs.append(pl.BlockSpec(memory_space=pl.ANY))
    return pl.pallas_call(
        functools.partial(_combine_kernel, tc=tc, has_norm=has_norm),
        out_shape=jax.ShapeDtypeStruct((m, d), F32),
        grid_spec=pltpu.PrefetchScalarGridSpec(
            num_scalar_prefetch=1,
            grid=(m // tc,),
            in_specs=in_specs,
            out_specs=pl.BlockSpec((tc, d), lambda i, sl: (i, 0)),
            scratch_shapes=[pltpu.VMEM((2, tc, d), F32), pltpu.SemaphoreType.DMA],
        ),
        compiler_params=_params("arbitrary"),
        name="moe_combine",
    )(*args)


def _moe_layer(x, gain, w_router, wg, wu, wd, final_gain, *, tm, hp=False):
    m, d = x.shape
    n_e = wg.shape[0]
    idx, wts = _router(x, gain, w_router)
    e_flat = idx[:, :2].reshape(-1)
    onehot = (e_flat[:, None] == jnp.arange(n_e, dtype=jnp.int32)[None, :]).astype(jnp.int32)
    csum = jnp.cumsum(onehot, axis=0)
    rank = jnp.sum(csum * onehot, axis=1) - 1
    tiles_e = (csum[-1] + tm - 1) // tm
    tile_end = jnp.cumsum(tiles_e)
    slot = ((tile_end - tiles_e)[e_flat] * tm + rank).astype(jnp.int32)
    n_tiles = (2 * m) // tm + n_e
    tile_expert = jnp.minimum(
        jnp.sum(jnp.arange(n_tiles, dtype=jnp.int32)[:, None] >= tile_end[None, :], axis=1), n_e - 1).astype(jnp.int32)
    n_active = tile_end[-1:].astype(jnp.int32)
    src_token = jnp.zeros((n_tiles * tm,), jnp.int32).at[slot].set(jnp.arange(2 * m, dtype=jnp.int32) // 2)
    xs = _gather_rows(x, src_token)
    ys = _moe_ffn(xs, gain, wg, wu, wd, tile_expert, n_active, tm=tm, hp=hp)
    return _combine(x, wts, ys, slot, final_gain)


def _gdn_weights(w_in, a_log, dt_bias):
    h_n = GDN_HEADS
    main = 2 * h_n * GDN_DK + 2 * h_n * GDN_DV
    w_gate = jnp.pad(w_in[:, main:], ((0, 0), (0, LANES - 2 * h_n)))
    pad = (h_n, LANES - 2 * h_n)
    alog_row = jnp.pad(a_log.astype(F32), pad).reshape(1, LANES)
    dtb_row = jnp.pad(dt_bias.astype(F32), pad).reshape(1, LANES)
    return w_in[:, :main], w_gate, alog_row, dtb_row


def _gdn_layer_prompt(x, gain, w_in, conv_w, a_log, dt_bias, o_gain, w_out, *, batch, seq):
    conv_dim = GDN_HEADS * (2 * GDN_DK + GDN_DV)
    w_main, w_gate, alog_row, dtb_row = _gdn_weights(w_in, a_log, dt_bias)
    proj = _matmul(x, w_main, gain=gain, tm=512, tn=2048, name="gdn_in_proj")
    gates = _matmul(x, w_gate, gain=gain, hp=True, tm=512, name="gdn_gate_proj")
    o, s_fin = _gdn_prompt(proj, gates, conv_w, alog_row, dtb_row, o_gain, batch=batch, seq=seq)
    y = _matmul(o, w_out, res=x, tm=512, name="gdn_out_proj")
    conv_state = proj.reshape(batch, seq, -1)[:, seq - (CONV_W - 1):, :conv_dim]
    return y, conv_state, s_fin


def _gdn_layer_step(x, gain, conv_prev, s_prev, w_in, conv_w, a_log, dt_bias, o_gain, w_out):
    w_main, w_gate, alog_row, dtb_row = _gdn_weights(w_in, a_log, dt_bias)
    proj = _matmul(x, w_main, gain=gain, hp=True, tn=1024, name="gdn_in_proj_s")
    gates = _matmul(x, w_gate, gain=gain, hp=True, name="gdn_gate_proj_s")
    o, conv_new, s_new = _gdn_step(proj, gates, conv_prev, s_prev, conv_w, alog_row, dtb_row, o_gain)
    y = _matmul(o.reshape(x.shape[0], -1), w_out, res=x, hp=True, name="gdn_out_proj_s")
    return y, conv_new, s_new


def _moba_layer_prompt(x, gain, w_in, w_out, *, batch, seq):
    proj = _matmul(x, w_in, gain=gain, tm=512, name="att_in_proj")
    cos_t, sin_a, sin_b = _rope_tables(jnp.arange(seq, dtype=jnp.int32))
    o, k_rot = _moba_prompt(proj, cos_t, sin_a, sin_b, batch=batch, seq=seq)
    y = _matmul(o, w_out, res=x, tm=512, name="att_out_proj")
    return y, k_rot, proj[:, (ATT_HEADS + KV_HEADS) * HEAD_DIM:]


def _moba_layer_step(x, gain, cache_k, cache_v, page_table, w_in, w_out):
    b = x.shape[0]
    hd = HEAD_DIM
    n_pool, page = cache_k.shape[0], cache_k.shape[1]
    past_len = page_table.shape[1] * page
    own = past_len // MOBA_BLOCK
    assert own * MOBA_BLOCK == past_len and own >= 1
    n_sel = min(MOBA_TOPK, own)
    ppb = MOBA_BLOCK // page
    proj = _matmul(x, w_in, gain=gain, hp=True, tn=1024, name="att_in_proj_s")
    ck = cache_k.reshape(n_pool, page * KV_HEADS, hd)
    cv = cache_v.reshape(n_pool, page * KV_HEADS, hd)
    kmeans = _cache_block_means(ck, page_table)
    cos_t, sin_a, sin_b = _rope_tables(jnp.full((1,), past_len, jnp.int32))
    q, k_rot, sel = _moba_select(proj, kmeans, cos_t, sin_a, sin_b, n_sel=n_sel)
    logical = sel[:, :, :n_sel, None] * ppb + jnp.arange(ppb, dtype=jnp.int32)
    pages = jnp.take_along_axis(page_table, logical.reshape(b, -1), axis=1).reshape(-1)
    v_new = proj[:, (ATT_HEADS + KV_HEADS) * hd:]
    o = _moba_step(q, k_rot.reshape(b, KV_HEADS, hd), v_new.reshape(b, KV_HEADS, hd), ck, cv, pages)
    y = _matmul(o, w_out, res=x, hp=True, name="att_out_proj_s")
    return y, k_rot.reshape(b, KV_HEADS * hd), v_new


def _norm_kernel(x_ref, g_ref, o_ref):
    o_ref[...] = _rms(x_ref[...], g_ref[...])


def _final_norm(x, gain, *, tm=512):
    m, d = x.shape
    tm = min(tm, m)
    return pl.pallas_call(
        _norm_kernel,
        out_shape=jax.ShapeDtypeStruct((m, d), F32),
        grid=(m // tm,),
        in_specs=[pl.BlockSpec((tm, d), lambda i: (i, 0)), pl.BlockSpec((1, d), lambda i: (0, 0))],
        out_specs=pl.BlockSpec((tm, d), lambda i: (i, 0)),
        compiler_params=_params("parallel"),
        name="final_norm",
    )(x, gain.reshape(1, d))


def kernel(x_prompt, x_sample, state_conv, state_delta, cache_k, cache_v, page_table, norm_mix, norm_ffn,
           norm_final, gdn_w_in, gdn_conv_w, gdn_a_log, gdn_dt_bias, gdn_o_norm, gdn_w_out, att_w_in,
           att_w_out, ffn_w_gate, ffn_w_up, ffn_w_down, moe_router, moe_w_gate, moe_w_up, moe_w_down):
    bp, seq, d = x_prompt.shape
    db, dec_seq, _ = x_sample.shape
    assert dec_seq == 1
    depth = norm_mix.shape[0]
    hp = x_prompt.reshape(bp * seq, d)
    hs = x_sample.reshape(db, d)
    conv_p, delta_p, k_p, v_p = [], [], [], []
    conv_s, delta_s, k_s, v_s = [], [], [], []
    for i in range(depth):
        j = i // 2
        last = i == depth - 1
        if i % 2 == 0:
            gdn = (gdn_w_in[j], gdn_conv_w[j], gdn_a_log[j], gdn_dt_bias[j], gdn_o_norm[j], gdn_w_out[j])
            hp, cp, sp = _gdn_layer_prompt(hp, norm_mix[i], *gdn, batch=bp, seq=seq)
            hs, cs, ss = _gdn_layer_step(hs, norm_mix[i], state_conv[j], state_delta[j], *gdn)
            conv_p.append(cp)
            delta_p.append(sp.astype(state_delta.dtype))
            conv_s.append(cs)
            delta_s.append(ss.astype(state_delta.dtype))
            ffn = (norm_ffn[i], ffn_w_gate[j], ffn_w_up[j], ffn_w_down[j])
            hp = _ffn(hp, *ffn)
            hs = _ffn(hs, *ffn, hp=True)
            if last:
                hp = _final_norm(hp, norm_final)
                hs = _final_norm(hs, norm_final)
        else:
            hp, kp, vp = _moba_layer_prompt(hp, norm_mix[i], att_w_in[j], att_w_out[j], batch=bp, seq=seq)
            hs, kn, vn = _moba_layer_step(hs, norm_mix[i], cache_k[j], cache_v[j], page_table,
                                          att_w_in[j], att_w_out[j])
            k_p.append(kp.reshape(bp, seq, KV_HEADS, HEAD_DIM))
            v_p.append(vp.reshape(bp, seq, KV_HEADS, HEAD_DIM))
            k_s.append(kn.reshape(db, 1, KV_HEADS, HEAD_DIM))
            v_s.append(vn.reshape(db, 1, KV_HEADS, HEAD_DIM))
            moe = (norm_ffn[i], moe_router[j], moe_w_gate[j], moe_w_up[j], moe_w_down[j])
            fg = norm_final if last else None
            hp = _moe_layer(hp, *moe, fg, tm=512)
            hs = _moe_layer(hs, *moe, fg, tm=32, hp=True)
    return (hp.reshape(bp, seq, d), hs.reshape(db, 1, d),
            jnp.stack(conv_p), jnp.stack(delta_p), jnp.stack(k_p), jnp.stack(v_p),
            jnp.stack(conv_s), jnp.stack(delta_s), jnp.stack(k_s), jnp.stack(v_s))
```

```python
import functools
import math

import jax
import jax.numpy as jnp
from jax import lax
from jax.experimental import pallas as pl
from jax.experimental.pallas import tpu as pltpu

F32 = jnp.float32
BF16 = jnp.bfloat16

GDN_HEADS = 8
GDN_DK = 128
GDN_DV = 128
CONV_W = 4
GDN_CHUNK = 64
ATT_HEADS = 8
KV_HEADS = 4
HEAD_DIM = 128
ROT_DIM = HEAD_DIM // 4
ROPE_THETA = 500000.0
MOBA_BLOCK = 256
MOBA_TOPK = 3
N_EXPERTS = 8
EPS = 1e-6

LANES = 128
SUBLANES = 8
VMEM_LIMIT = 56 * 1024 * 1024

NEG_INF = float("-inf")
_ISSUE_UNROLL = 8


def _params(*sem):
    return pltpu.CompilerParams(dimension_semantics=sem, vmem_limit_bytes=VMEM_LIMIT)


def _rms(x, gain):
    return x * lax.rsqrt(jnp.mean(x * x, axis=-1, keepdims=True) + EPS) * gain


def _silu(x):
    return x * jax.nn.sigmoid(x)


def _softplus(x):
    return jnp.maximum(x, 0.0) + jnp.log1p(jnp.exp(-jnp.abs(x)))


def _split2(a):
    hi = a.astype(BF16)
    lo = (a - hi.astype(F32)).astype(BF16)
    return hi, lo


def _split3(a):
    hi = a.astype(BF16)
    r = a - hi.astype(F32)
    mid = r.astype(BF16)
    lo = (r - mid.astype(F32)).astype(BF16)
    return hi, mid, lo


_NN = (((1,), (0,)), ((), ()))
_NT = (((1,), (1,)), ((), ()))
_TN = (((0,), (0,)), ((), ()))


def _dot(a, b, dims=_NN):
    return lax.dot_general(a, b, dims, preferred_element_type=F32)


def _dot_hp(a, b, dims=_NN):
    ah, al = _split2(a)
    bh, bl = _split2(b)
    return _dot(ah, bh, dims) + (_dot(al, bh, dims) + _dot(ah, bl, dims))


def _mm_kernel(*refs, has_gain, has_res, hp):
    it = iter(refs)
    a_ref = next(it)
    g_ref = next(it) if has_gain else None
    w_ref = next(it)
    r_ref = next(it) if has_res else None
    o_ref = next(it)
    an_ref = next(it)

    @pl.when(pl.program_id(1) == 0)
    def _():
        a = a_ref[...].astype(F32)
        if has_gain:
            a = _rms(a, g_ref[...])
        an_ref[...] = a.astype(an_ref.dtype)

    acc = (_dot_hp if hp else _dot)(an_ref[...], w_ref[...])
    if has_res:
        acc = acc + r_ref[...]
    o_ref[...] = acc.astype(o_ref.dtype)


def _matmul(a, w, *, gain=None, res=None, hp=False, tm=512, tn=None, out_dtype=F32, name="matmul"):
    m, k = a.shape
    n = w.shape[1]
    tm = min(tm, m)
    tn = n if tn is None else min(tn, n)
    assert m % tm == 0 and n % tn == 0
    args = [a]
    in_specs = [pl.BlockSpec((tm, k), lambda i, j: (i, 0))]
    if gain is not None:
        args.append(gain.reshape(1, k).astype(F32))
        in_specs.append(pl.BlockSpec((1, k), lambda i, j: (0, 0)))
    args.append(w if hp else w.astype(BF16))
    in_specs.append(pl.BlockSpec((k, tn), lambda i, j: (0, j)))
    if res is not None:
        args.append(res)
        in_specs.append(pl.BlockSpec((tm, tn), lambda i, j: (i, j)))
    scratch = [pltpu.VMEM((tm, k), F32 if hp else BF16)]
    return pl.pallas_call(
        functools.partial(_mm_kernel, has_gain=gain is not None, has_res=res is not None, hp=hp),
        out_shape=jax.ShapeDtypeStruct((m, n), out_dtype),
        grid=(m // tm, n // tn),
        in_specs=in_specs,
        out_specs=pl.BlockSpec((tm, tn), lambda i, j: (i, j)),
        scratch_shapes=scratch,
        compiler_params=_params("parallel", "arbitrary"),
        name=name,
    )(*args)


def _swiglu_step(xn, wg, wu, wd, hp):
    dot = _dot_hp if hp else _dot
    act = _silu(dot(xn, wg)) * dot(xn, wu)
    return dot(act if hp else act.astype(BF16), wd)


def _ffn_kernel(x_ref, g_ref, wg_ref, wu_ref, wd_ref, o_ref, xn_ref, acc_ref, *, hp):
    f = pl.program_id(1)

    @pl.when(f == 0)
    def _():
        xn_ref[...] = _rms(x_ref[...], g_ref[...]).astype(xn_ref.dtype)
        acc_ref[...] = jnp.zeros_like(acc_ref)

    acc_ref[...] += _swiglu_step(xn_ref[...], wg_ref[...], wu_ref[...], wd_ref[...], hp)

    @pl.when(f == pl.num_programs(1) - 1)
    def _():
        o_ref[...] = x_ref[...] + acc_ref[...]


def _ffn(x, gain, wg, wu, wd, *, hp=False, tm=1024, tf=512):
    m, d = x.shape
    dff = wg.shape[1]
    tm = min(tm, m)
    assert m % tm == 0 and dff % tf == 0
    wdt = F32 if hp else BF16
    return pl.pallas_call(
        functools.partial(_ffn_kernel, hp=hp),
        out_shape=jax.ShapeDtypeStruct((m, d), F32),
        grid=(m // tm, dff // tf),
        in_specs=[
            pl.BlockSpec((tm, d), lambda i, f: (i, 0)),
            pl.BlockSpec((1, d), lambda i, f: (0, 0)),
            pl.BlockSpec((d, tf), lambda i, f: (0, f)),
            pl.BlockSpec((d, tf), lambda i, f: (0, f)),
            pl.BlockSpec((tf, d), lambda i, f: (f, 0)),
        ],
        out_specs=pl.BlockSpec((tm, d), lambda i, f: (i, 0)),
        scratch_shapes=[pltpu.VMEM((tm, d), wdt), pltpu.VMEM((tm, d), F32)],
        compiler_params=_params("parallel", "arbitrary"),
        name="ffn",
    )(x, gain.reshape(1, d), wg.astype(wdt), wu.astype(wdt), wd.astype(wdt))


def _bdot(a, b, dims=_NN):
    return _dot(a.astype(BF16), b.astype(BF16), dims)


def _neumann_inverse_minus_eye(mats):
    c = mats[0].shape[0]
    ps = [-a for a in mats]
    ns = list(ps)
    for _ in range(int(math.log2(c)) - 1):
        ps = [_bdot(p, p) for p in ps]
        ns = [n + p + _bdot(n, p) for n, p in zip(ns, ps)]
    return ns


def _gdn_kernel(proj_ref, gate_ref, convw_ref, alog_ref, dtb_ref, ogain_ref,
                o_ref, sfin_ref, s_ref, tail_ref, *, tt, c):
    h_n, dk, dv = GDN_HEADS, GDN_DK, GDN_DV
    t = pl.program_id(1)

    @pl.when(t == 0)
    def _():
        s_ref[...] = jnp.zeros_like(s_ref)
        tail_ref[...] = jnp.zeros_like(tail_ref)

    gt = gate_ref[...]
    beta_all = jax.nn.sigmoid(gt)
    g_all = -jnp.exp(alog_ref[...]) * _softplus(gt + dtb_ref[...])

    ri = lax.broadcasted_iota(jnp.int32, (c, c), 0)
    ci = lax.broadcasted_iota(jnp.int32, (c, c), 1)
    tri = ri >= ci
    strict = ri > ci
    tri_b = tri.astype(BF16)
    ri2 = lax.broadcasted_iota(jnp.int32, (c, LANES), 0)
    ci2 = lax.broadcasted_iota(jnp.int32, (c, LANES), 1)
    upper_b = (ri2 <= ci2).astype(BF16)

    n_chunks = tt // c
    gams, gam_ts = [], []
    for ck in range(n_chunks):
        parts = _split3(g_all[ck * c:(ck + 1) * c, :])
        gam = sum(_dot(tri_b, p) for p in parts)
        gam_t = sum(_dot(p, upper_b, _TN) for p in parts)
        gams.append(gam)
        gam_ts.append(gam_t)

    def conv_silu(off):
        raw = proj_ref[:, off:off + LANES]
        xs = jnp.concatenate([tail_ref[:, off:off + LANES], raw], axis=0)
        w = convw_ref[:, off:off + LANES]
        acc = raw * w[CONV_W - 1:CONV_W]
        for j in range(CONV_W - 1):
            lo = SUBLANES - (CONV_W - 1) + j
            acc = acc + xs[lo:lo + tt] * w[j:j + 1]
        return _silu(acc)

    def l2n(x):
        return x * lax.rsqrt(jnp.sum(x * x, axis=-1, keepdims=True) + 1e-6)

    heads = range(h_n)
    pairs = [(ck, h) for ck in range(n_chunks) for h in heads]
    z_off = 2 * h_n * dk + h_n * dv
    q_all = [l2n(conv_silu(h * dk)) * (dk ** -0.5) for h in heads]
    k_all = [l2n(conv_silu(h_n * dk + h * dk)) for h in heads]
    v_all = [conv_silu(2 * h_n * dk + h * dv) for h in heads]

    def rows(x, ck):
        return x[ck * c:(ck + 1) * c]

    qc = [rows(q_all[h], ck) for ck, h in pairs]
    kc = [rows(k_all[h], ck) for ck, h in pairs]
    bcol = [rows(beta_all, ck)[:, h:h + 1] for ck, h in pairs]
    col = [gams[ck][:, h_n + h:h_n + h + 1] for ck, h in pairs]
    dm = [jnp.exp(jnp.where(tri, cl - gam_ts[ck][h_n + h:h_n + h + 1, :c], NEG_INF))
          for cl, (ck, h) in zip(col, pairs)]
    kb = [x * b for x, b in zip(kc, bcol)]
    kq = [_bdot(jnp.concatenate([x, y], axis=0), z, _NT) for x, y, z in zip(kb, qc, kc)]
    a = [jnp.where(strict, x[:c] * d, 0.0) for x, d in zip(kq, dm)]
    attn = [(x[c:] * d).astype(BF16) for x, d in zip(kq, dm)]
    ninv = _neumann_inverse_minus_eye(a)
    eg = [jnp.exp(cl) for cl in col]
    rhs = [jnp.concatenate([rows(v_all[h], ck) * b, x * e], axis=1)
           for (ck, h), b, x, e in zip(pairs, bcol, kb, eg)]
    sol = [r + _bdot(n, r) for n, r in zip(ninv, rhs)]
    glast = [cl[c - 1:c, :] for cl in col]
    lhs = [jnp.concatenate([x[:, dv:], y * e], axis=0).astype(BF16)
           for x, y, e in zip(sol, qc, eg)]
    k_dec = [(x * jnp.exp(g - cl)).astype(BF16) for x, g, cl in zip(kc, glast, col)]
    g_end = [jnp.exp(g) for g in glast]

    ogain = ogain_ref[...]
    for ck in range(n_chunks):
        idx = [ck * h_n + h for h in heads]
        s_old = [s_ref[h] for h in heads]
        both = [_dot(lhs[i], s.astype(BF16)) for i, s in zip(idx, s_old)]
        ub = [(sol[i][:, :dv] - b[:c]).astype(BF16) for i, b in zip(idx, both)]
        o = [b[c:] + _dot(attn[i], u) for i, b, u in zip(idx, both, ub)]
        for h, i, s, u in zip(heads, idx, s_old, ub):
            s_ref[h] = s * g_end[i] + _dot(k_dec[i], u, _TN)
        for h, x in zip(heads, o):
            zc = proj_ref[ck * c:(ck + 1) * c, z_off + h * dv:z_off + (h + 1) * dv]
            o_ref[ck * c:(ck + 1) * c, h * dv:(h + 1) * dv] = (_rms(x, ogain) * _silu(zc)).astype(o_ref.dtype)

    tail_ref[...] = proj_ref[tt - SUBLANES:tt, :tail_ref.shape[1]]

    @pl.when(t == pl.num_programs(1) - 1)
    def _():
        sfin_ref[0] = s_ref[...]


def _gdn_prompt(proj, gates, conv_w, alog_row, dtb_row, o_gain, *, batch, seq, tt=256):
    h_n, dk, dv = GDN_HEADS, GDN_DK, GDN_DV
    conv_dim = h_n * (2 * dk + dv)
    width = proj.shape[1]
    tt = min(tt, seq)
    c = math.gcd(seq, GDN_CHUNK)
    assert seq % tt == 0 and tt % c == 0 and tt >= SUBLANES
    nt = seq // tt
    return pl.pallas_call(
        functools.partial(_gdn_kernel, tt=tt, c=c),
        out_shape=(jax.ShapeDtypeStruct((batch * seq, h_n * dv), BF16),
                   jax.ShapeDtypeStruct((batch, h_n, dk, dv), F32)),
        grid=(batch, nt),
        in_specs=[
            pl.BlockSpec((tt, width), lambda b, t: (b * nt + t, 0)),
            pl.BlockSpec((tt, LANES), lambda b, t: (b * nt + t, 0)),
            pl.BlockSpec((CONV_W, conv_dim), lambda b, t: (0, 0)),
            pl.BlockSpec((1, LANES), lambda b, t: (0, 0)),
            pl.BlockSpec((1, LANES), lambda b, t: (0, 0)),
            pl.BlockSpec((1, dv), lambda b, t: (0, 0)),
        ],
        out_specs=(pl.BlockSpec((tt, h_n * dv), lambda b, t: (b * nt + t, 0)),
                   pl.BlockSpec((1, h_n, dk, dv), lambda b, t: (b, 0, 0, 0))),
        scratch_shapes=[pltpu.VMEM((h_n, dk, dv), F32), pltpu.VMEM((SUBLANES, conv_dim), F32)],
        compiler_params=_params("parallel", "arbitrary"),
        name="gdn_prompt",
    )(proj, gates, conv_w, alog_row, dtb_row, o_gain.reshape(1, dv))


def _gdn_step_kernel(proj_ref, gate_ref, cprev_ref, s_ref, convw_ref, alog_ref, dtb_ref, ogain_ref,
                     o_ref, cnew_ref, snew_ref):
    h_n, dk, dv = GDN_HEADS, GDN_DK, GDN_DV
    conv_dim = h_n * (2 * dk + dv)
    u_row = proj_ref[0]
    prev = cprev_ref[0]
    w = convw_ref[...]
    raw = u_row[:, :conv_dim]
    acc = raw * w[CONV_W - 1:CONV_W]
    for j in range(CONV_W - 1):
        acc = acc + prev[j:j + 1] * w[j:j + 1]
    qkv = _silu(acc)
    for j in range(CONV_W - 2):
        cnew_ref[0, j:j + 1, :] = prev[j + 1:j + 2]
    cnew_ref[0, CONV_W - 2:CONV_W - 1, :] = raw

    gt = gate_ref[0]
    beta_all = jax.nn.sigmoid(gt)
    g_all = -jnp.exp(alog_ref[...]) * _softplus(gt + dtb_ref[...])
    rows = lax.broadcasted_iota(jnp.int32, (SUBLANES, LANES), 0)
    ogain = ogain_ref[...]

    def l2n(x):
        return x * lax.rsqrt(jnp.sum(x * x, axis=-1, keepdims=True) + 1e-6)

    def as_row0(x, y=None):
        out = jnp.where(rows == 0, jnp.broadcast_to(x, (SUBLANES, LANES)), 0.0)
        if y is not None:
            out = jnp.where(rows == 1, jnp.broadcast_to(y, (SUBLANES, LANES)), out)
        return out

    for h in range(h_n):
        q = l2n(qkv[:, h * dk:(h + 1) * dk]) * (dk ** -0.5)
        k = l2n(qkv[:, h_n * dk + h * dk:h_n * dk + (h + 1) * dk])
        v = qkv[:, 2 * h_n * dk + h * dv:2 * h_n * dk + (h + 1) * dv]
        z = u_row[:, conv_dim + h * dv:conv_dim + (h + 1) * dv]
        beta = beta_all[:, h:h + 1]
        eg = jnp.exp(g_all[:, h_n + h:h_n + h + 1])
        s = s_ref[0, h]
        kcd = k * (beta * eg)
        q_dec = q * eg
        both = _dot_hp(as_row0(kcd, q_dec), s)
        u = v * beta - both[0:1]
        o = both[1:2] + jnp.sum(q * k, axis=-1, keepdims=True) * u
        snew_ref[0, h] = s * eg + _dot_hp(as_row0(k), as_row0(u), _TN)
        o_ref[0, :, h * dv:(h + 1) * dv] = (_rms(o, ogain) * _silu(z)).astype(o_ref.dtype)


def _gdn_step(proj, gates, conv_prev, s_prev, conv_w, alog_row, dtb_row, o_gain):
    h_n, dk, dv = GDN_HEADS, GDN_DK, GDN_DV
    conv_dim = h_n * (2 * dk + dv)
    b, width = proj.shape
    return pl.pallas_call(
        _gdn_step_kernel,
        out_shape=(jax.ShapeDtypeStruct((b, 1, h_n * dv), F32),
                   jax.ShapeDtypeStruct((b, CONV_W - 1, conv_dim), F32),
                   jax.ShapeDtypeStruct((b, h_n, dk, dv), F32)),
        grid=(b,),
        in_specs=[
            pl.BlockSpec((1, 1, width), lambda i: (i, 0, 0)),
            pl.BlockSpec((1, 1, LANES), lambda i: (i, 0, 0)),
            pl.BlockSpec((1, CONV_W - 1, conv_dim), lambda i: (i, 0, 0)),
            pl.BlockSpec((1, h_n, dk, dv), lambda i: (i, 0, 0, 0)),
            pl.BlockSpec((CONV_W, conv_dim), lambda i: (0, 0)),
            pl.BlockSpec((1, LANES), lambda i: (0, 0)),
            pl.BlockSpec((1, LANES), lambda i: (0, 0)),
            pl.BlockSpec((1, dv), lambda i: (0, 0)),
        ],
        out_specs=(pl.BlockSpec((1, 1, h_n * dv), lambda i: (i, 0, 0)),
                   pl.BlockSpec((1, CONV_W - 1, conv_dim), lambda i: (i, 0, 0)),
                   pl.BlockSpec((1, h_n, dk, dv), lambda i: (i, 0, 0, 0))),
        compiler_params=_params("parallel"),
        name="gdn_step",
    )(proj.reshape(b, 1, width), gates.reshape(b, 1, LANES), conv_prev, s_prev, conv_w,
      alog_row, dtb_row, o_gain.reshape(1, dv))


def _rope(x, cos, sin_a, sin_b):
    half = ROT_DIM // 2
    return (x * cos + pltpu.roll(x, HEAD_DIM - half, 1) * sin_a + pltpu.roll(x, half, 1) * sin_b)


def _rope_tables(pos):
    half = ROT_DIM // 2
    inv = jnp.power(ROPE_THETA, -jnp.arange(half, dtype=F32) * (2.0 / ROT_DIM))
    ang = pos.astype(F32)[:, None] * inv[None, :]
    cos, sin = jnp.cos(ang), jnp.sin(ang)
    n = pos.shape[0]
    rest = HEAD_DIM - ROT_DIM
    cos_t = jnp.concatenate([cos, cos, jnp.ones((n, rest), F32)], axis=1)
    sin_a = jnp.concatenate([-sin, jnp.zeros((n, half + rest), F32)], axis=1)
    sin_b = jnp.concatenate([jnp.zeros((n, half), F32), sin, jnp.zeros((n, rest), F32)], axis=1)
    return cos_t, sin_a, sin_b


def _moba_kernel(q_ref, k_ref, v_ref, cos_ref, sa_ref, sb_ref, o_ref, kout_ref, kb_ref, vb_ref, *, seq):
    blk, hd = MOBA_BLOCK, HEAD_DIM
    group = ATT_HEADS // KV_HEADS
    nb = seq // blk
    nq = group * blk
    scale = hd ** -0.5

    kout_ref[...] = _rope(k_ref[...], cos_ref[...], sa_ref[...], sb_ref[...])
    kb_ref[...] = kout_ref[...].astype(BF16)
    vb_ref[...] = v_ref[...].astype(BF16)
    kmean = jnp.concatenate(
        [jnp.sum(kout_ref[n * blk:(n + 1) * blk, :], axis=0, keepdims=True) for n in range(nb)],
        axis=0) * (1.0 / blk)

    blk_id = lax.broadcasted_iota(jnp.int32, (nb, nq), 0)
    key_r = lax.broadcasted_iota(jnp.int32, (blk, nq), 0)
    qry_c = lax.broadcasted_iota(jnp.int32, (blk, nq), 1) % blk
    causal = key_r <= qry_c

    for i in range(nb):
        rows = slice(i * blk, (i + 1) * blk)
        cos, sa, sb = cos_ref[rows, :], sa_ref[rows, :], sb_ref[rows, :]
        qs = [_rope(q_ref[rows, g * hd:(g + 1) * hd], cos, sa, sb) for g in range(group)]
        q = jnp.concatenate(qs, axis=0)
        qb = q.astype(BF16)

        if i > MOBA_TOPK:
            gate = jnp.where(blk_id < i, _dot_hp(kmean, q, _NT), NEG_INF)
            rank = jnp.zeros((nb, nq), F32)
            for m in range(i):
                gm = gate[m:m + 1, :]
                rank = rank + jnp.where((gm > gate) | ((gm == gate) & (m < blk_id)), 1.0, 0.0)
            sel = jnp.where((blk_id < i) & (rank < MOBA_TOPK), 1.0, 0.0)
        else:
            sel = None

        ss = []
        for j in range(i + 1):
            s = _dot(kb_ref[j * blk:(j + 1) * blk, :], qb, _NT) * scale
            if j == i:
                s = jnp.where(causal, s, NEG_INF)
            elif sel is not None:
                s = jnp.where(sel[j:j + 1, :] > 0.0, s, NEG_INF)
            ss.append(s)
        m_all = functools.reduce(jnp.maximum, [jnp.max(s, axis=0, keepdims=True) for s in ss])
        ps = [jnp.exp(s - m_all) for s in ss]
        l_all = functools.reduce(jnp.add, [jnp.sum(p, axis=0, keepdims=True) for p in ps])
        acc = functools.reduce(
            jnp.add, [_dot(vb_ref[j * blk:(j + 1) * blk, :], p.astype(BF16), _TN) for j, p in enumerate(ps)])
        out = (acc / l_all).T
        for g in range(group):
            o_ref[rows, g * hd:(g + 1) * hd] = out[g * blk:(g + 1) * blk].astype(o_ref.dtype)


def _moba_prompt(proj, cos_t, sin_a, sin_b, *, batch, seq):
    hd = HEAD_DIM
    group = ATT_HEADS // KV_HEADS
    assert seq % MOBA_BLOCK == 0
    nb = seq // MOBA_BLOCK
    tbl = pl.BlockSpec((seq, hd), lambda b, g: (0, 0))
    return pl.pallas_call(
        functools.partial(_moba_kernel, seq=seq),
        out_shape=(jax.ShapeDtypeStruct((batch * seq, ATT_HEADS * hd), BF16),
                   jax.ShapeDtypeStruct((batch * seq, KV_HEADS * hd), F32)),
        grid=(batch, KV_HEADS),
        in_specs=[
            pl.BlockSpec((seq, group * hd), lambda b, g: (b, g)),
            pl.BlockSpec((seq, hd), lambda b, g: (b, ATT_HEADS + g)),
            pl.BlockSpec((seq, hd), lambda b, g: (b, ATT_HEADS + KV_HEADS + g)),
            tbl, tbl, tbl,
        ],
        out_specs=(pl.BlockSpec((seq, group * hd), lambda b, g: (b, g)),
                   pl.BlockSpec((seq, hd), lambda b, g: (b, g))),
        scratch_shapes=[pltpu.VMEM((seq, hd), BF16), pltpu.VMEM((seq, hd), BF16)],
        compiler_params=_params("parallel", "parallel"),
        name="moba_prompt",
    )(proj, proj, proj, cos_t, sin_a, sin_b)


def _kmean_kernel(pt_ref, *refs, pages, pages_per_block):
    page_refs, o_ref = refs[:pages], refs[pages]
    rows = page_refs[0].shape[1]
    tokens = pages_per_block * rows // KV_HEADS
    for n in range(pages // pages_per_block):
        part = jnp.zeros((SUBLANES, HEAD_DIM), F32)
        for p in range(pages_per_block):
            x = page_refs[n * pages_per_block + p][0]
            part = part + jnp.sum(x.reshape(rows // SUBLANES, SUBLANES, HEAD_DIM), axis=0)
        acc = part[:KV_HEADS]
        for r in range(1, SUBLANES // KV_HEADS):
            acc = acc + part[r * KV_HEADS:(r + 1) * KV_HEADS]
        o_ref[0, n] = acc * (1.0 / tokens)


def _cache_block_means(cache, page_table, *, blocks_per_step=8):
    n_pool, rows, hd = cache.shape
    page = rows // KV_HEADS
    b, n_pages = page_table.shape
    ppb = MOBA_BLOCK // page
    assert ppb * page == MOBA_BLOCK and n_pages % ppb == 0 and rows % SUBLANES == 0
    n_blocks = n_pages // ppb
    bps = math.gcd(blocks_per_step, n_blocks)
    pages = bps * ppb
    specs = [pl.BlockSpec((1, rows, hd), lambda i, s, pt, p=p: (pt[i, s * pages + p], 0, 0))
             for p in range(pages)]
    return pl.pallas_call(
        functools.partial(_kmean_kernel, pages=pages, pages_per_block=ppb),
        out_shape=jax.ShapeDtypeStruct((b, n_blocks, KV_HEADS, hd), F32),
        grid_spec=pltpu.PrefetchScalarGridSpec(
            num_scalar_prefetch=1,
            grid=(b, n_blocks // bps),
            in_specs=specs,
            out_specs=pl.BlockSpec((1, bps, KV_HEADS, hd), lambda i, s, pt: (i, s, 0, 0)),
        ),
        compiler_params=_params("parallel", "arbitrary"),
        name="cache_block_means",
    )(page_table, *([cache] * pages))


def _moba_select_kernel(proj_ref, km_ref, cos_ref, sa_ref, sb_ref, q_ref, k_ref, sel_ref, *, n_sel):
    hd = HEAD_DIM
    group = ATT_HEADS // KV_HEADS
    row = proj_ref[0]
    cos, sa, sb = cos_ref[...], sa_ref[...], sb_ref[...]
    rows = lax.broadcasted_iota(jnp.int32, (SUBLANES, hd), 0)
    q = jnp.zeros((SUBLANES, hd), F32)
    for h in range(ATT_HEADS):
        qh = _rope(row[:, h * hd:(h + 1) * hd], cos, sa, sb)
        q = jnp.where(rows == h, jnp.broadcast_to(qh, (SUBLANES, hd)), q)
    q_ref[0] = q
    for g in range(KV_HEADS):
        off = (ATT_HEADS + g) * hd
        k_ref[0, :, g * hd:(g + 1) * hd] = _rope(row[:, off:off + hd], cos, sa, sb)

    nb = km_ref.shape[1]
    hrow = lax.broadcasted_iota(jnp.int32, (ATT_HEADS, nb), 0)
    lane = lax.broadcasted_iota(jnp.int32, (ATT_HEADS, nb), 1)
    gate = jnp.zeros((ATT_HEADS, nb), F32)
    for g in range(KV_HEADS):
        gg = _dot_hp(q, km_ref[0, :, g, :], _NT)
        gate = jnp.where(hrow // group == g, gg, gate)
    out_lane = lax.broadcasted_iota(jnp.int32, (ATT_HEADS, LANES), 1)
    sel = jnp.zeros((ATT_HEADS, LANES), jnp.int32)
    for t in range(n_sel):
        m = jnp.max(gate, axis=-1, keepdims=True)
        idx = jnp.min(jnp.where(gate == m, lane, nb), axis=-1, keepdims=True)
        sel = jnp.where(out_lane == t, idx, sel)
        gate = jnp.where(lane == idx, NEG_INF, gate)
    sel_ref[0] = sel


def _moba_select(proj, kmeans, cos_t, sin_a, sin_b, *, n_sel):
    b, width = proj.shape
    nb = kmeans.shape[1]
    hd = HEAD_DIM
    tbl = pl.BlockSpec((1, hd), lambda i: (0, 0))
    return pl.pallas_call(
        functools.partial(_moba_select_kernel, n_sel=n_sel),
        out_shape=(jax.ShapeDtypeStruct((b, ATT_HEADS, hd), F32),
                   jax.ShapeDtypeStruct((b, 1, KV_HEADS * hd), F32),
                   jax.ShapeDtypeStruct((b, ATT_HEADS, LANES), jnp.int32)),
        grid=(b,),
        in_specs=[pl.BlockSpec((1, 1, width), lambda i: (i, 0, 0)),
                  pl.BlockSpec((1, nb, KV_HEADS, hd), lambda i: (i, 0, 0, 0)),
                  tbl, tbl, tbl],
        out_specs=(pl.BlockSpec((1, ATT_HEADS, hd), lambda i: (i, 0, 0)),
                   pl.BlockSpec((1, 1, KV_HEADS * hd), lambda i: (i, 0, 0)),
                   pl.BlockSpec((1, ATT_HEADS, LANES), lambda i: (i, 0, 0))),
        compiler_params=_params("parallel"),
        name="moba_select",
    )(proj.reshape(b, 1, width), kmeans, cos_t, sin_a, sin_b)


def _moba_step_kernel(pg_ref, q_ref, kn_ref, vn_ref, *refs, n_pages):
    k_refs, v_refs, o_ref = refs[:n_pages], refs[n_pages:2 * n_pages], refs[2 * n_pages]
    hd = HEAD_DIM
    group = ATT_HEADS // KV_HEADS
    scale = hd ** -0.5
    h = pl.program_id(1)
    g = h // group
    rows = k_refs[0].shape[1]
    q = q_ref[0, pl.ds(h, 1), :]
    q8 = jnp.broadcast_to(q, (SUBLANES, hd))
    lane_kvh = lax.broadcasted_iota(jnp.int32, (1, rows), 1) % KV_HEADS
    mine = lane_kvh == g
    kn = kn_ref[0, pl.ds(g, 1), :]
    vn = vn_ref[0, pl.ds(g, 1), :]
    s_self = jnp.sum(q * kn, axis=-1, keepdims=True) * scale
    scores = []
    m = s_self
    for p in range(n_pages):
        s = _dot_hp(q8, k_refs[p][0], _NT)[0:1] * scale
        s = jnp.where(mine, s, NEG_INF)
        scores.append(s)
        m = jnp.maximum(m, jnp.max(s, axis=-1, keepdims=True))
    p_self = jnp.exp(s_self - m)
    l = p_self
    acc = p_self * vn
    for p in range(n_pages):
        e = jnp.exp(scores[p] - m)
        l = l + jnp.sum(e, axis=-1, keepdims=True)
        acc = acc + _dot_hp(jnp.broadcast_to(e, (SUBLANES, rows)), v_refs[p][0])[0:1]
    o_ref[0] = (acc / l).astype(o_ref.dtype)


def _moba_step(q, k_new, v_new, cache_k, cache_v, pages):
    b = q.shape[0]
    hd = HEAD_DIM
    rows = cache_k.shape[1]
    n_pages = pages.shape[0] // (b * ATT_HEADS)

    def page_spec(p):
        return pl.BlockSpec((1, rows, hd), lambda i, h, pg, p=p: (pg[(i * ATT_HEADS + h) * n_pages + p], 0, 0))

    specs = ([pl.BlockSpec((1, ATT_HEADS, hd), lambda i, h, pg: (i, 0, 0)),
              pl.BlockSpec((1, KV_HEADS, hd), lambda i, h, pg: (i, 0, 0)),
              pl.BlockSpec((1, KV_HEADS, hd), lambda i, h, pg: (i, 0, 0))]
             + [page_spec(p) for p in range(n_pages)] * 2)
    out = pl.pallas_call(
        functools.partial(_moba_step_kernel, n_pages=n_pages),
        out_shape=jax.ShapeDtypeStruct((b * ATT_HEADS, 1, hd), F32),
        grid_spec=pltpu.PrefetchScalarGridSpec(
            num_scalar_prefetch=1,
            grid=(b, ATT_HEADS),
            in_specs=specs,
            out_specs=pl.BlockSpec((1, 1, hd), lambda i, h, pg: (i * ATT_HEADS + h, 0, 0)),
        ),
        compiler_params=_params("parallel", "arbitrary"),
        name="moba_step",
    )(pages, q, k_new, v_new, *([cache_k] * n_pages), *([cache_v] * n_pages))
    return out.reshape(b, ATT_HEADS * hd)


def _router_kernel(x_ref, g_ref, w_ref, idx_ref, wts_ref):
    logits = _dot_hp(_rms(x_ref[...], g_ref[...]), w_ref[...])
    lane = lax.broadcasted_iota(jnp.int32, logits.shape, 1)
    logits = jnp.where(lane < N_EXPERTS, logits, NEG_INF)
    m1 = jnp.max(logits, axis=-1, keepdims=True)
    i1 = jnp.min(jnp.where(logits == m1, lane, LANES), axis=-1, keepdims=True)
    rest = jnp.where(lane == i1, NEG_INF, logits)
    m2 = jnp.max(rest, axis=-1, keepdims=True)
    i2 = jnp.min(jnp.where(rest == m2, lane, LANES), axis=-1, keepdims=True)
    e = jnp.exp(m2 - m1)
    w1 = 1.0 / (1.0 + e)
    idx_ref[...] = jnp.where(lane == 0, i1, jnp.where(lane == 1, i2, 0))
    wts_ref[...] = jnp.where(lane == 0, w1, jnp.where(lane == 1, e * w1, 0.0))


def _router(x, gain, w_router, *, tm=512):
    m, d = x.shape
    tm = min(tm, m)
    assert m % tm == 0
    w = jnp.pad(w_router, ((0, 0), (0, LANES - w_router.shape[1])))
    return pl.pallas_call(
        _router_kernel,
        out_shape=(jax.ShapeDtypeStruct((m, LANES), jnp.int32), jax.ShapeDtypeStruct((m, LANES), F32)),
        grid=(m // tm,),
        in_specs=[pl.BlockSpec((tm, d), lambda i: (i, 0)), pl.BlockSpec((1, d), lambda i: (0, 0)),
                  pl.BlockSpec((d, LANES), lambda i: (0, 0))],
        out_specs=(pl.BlockSpec((tm, LANES), lambda i: (i, 0)), pl.BlockSpec((tm, LANES), lambda i: (i, 0))),
        compiler_params=_params("parallel"),
        name="router",
    )(x, gain.reshape(1, d), w)


def _row_copy(src, dst, sem, i, j):
    return pltpu.make_async_copy(src.at[pl.ds(i, 1)], dst.at[pl.ds(j, 1)], sem)


def _moe_ffn_kernel(te_ref, na_ref, src_ref, x_ref, g_ref, wg_ref, wu_ref, wd_ref, o_ref,
                    xbuf_ref, xn_ref, acc_ref, sem, *, tm, hp):
    t, f = pl.program_id(0), pl.program_id(1)
    last = f == pl.num_programs(1) - 1
    n_active = na_ref[0]
    active = t < n_active
    slot = t % 2

    def start_gather(tile, dst_slot):
        def issue(r, carry):
            _row_copy(x_ref, xbuf_ref.at[dst_slot], sem.at[dst_slot], src_ref[tile * tm + r], r).start()
            return carry

        lax.fori_loop(0, tm, issue, 0, unroll=_ISSUE_UNROLL)

    @pl.when(active)
    def _():
        @pl.when(f == 0)
        def _():
            @pl.when(t == 0)
            def _():
                start_gather(0, 0)

            pltpu.make_async_copy(x_ref.at[pl.ds(0, tm)], xbuf_ref.at[slot], sem.at[slot]).wait()
            xn_ref[...] = _rms(xbuf_ref[slot], g_ref[...]).astype(xn_ref.dtype)
            acc_ref[...] = jnp.zeros_like(acc_ref)

            @pl.when(t + 1 < n_active)
            def _():
                start_gather(t + 1, 1 - slot)

        acc_ref[...] += _swiglu_step(xn_ref[...], wg_ref[0], wu_ref[0], wd_ref[0], hp)

        @pl.when(last)
        def _():
            o_ref[...] = acc_ref[...]

    @pl.when(jnp.logical_and(jnp.logical_not(active), last))
    def _():
        o_ref[...] = jnp.zeros_like(o_ref)


def _moe_ffn(x, src_token, gain, wg, wu, wd, tile_expert, n_active, *, tm, tf, hp=False):
    d = x.shape[1]
    s = src_token.shape[0]
    dff = wg.shape[2]
    assert s % tm == 0 and dff % tf == 0
    wdt = F32 if hp else BF16
    return pl.pallas_call(
        functools.partial(_moe_ffn_kernel, tm=tm, hp=hp),
        out_shape=jax.ShapeDtypeStruct((s, d), F32),
        grid_spec=pltpu.PrefetchScalarGridSpec(
            num_scalar_prefetch=3,
            grid=(s // tm, dff // tf),
            in_specs=[
                pl.BlockSpec(memory_space=pl.ANY),
                pl.BlockSpec((1, d), lambda t, f, te, na, src: (0, 0)),
                pl.BlockSpec((1, d, tf), lambda t, f, te, na, src: (te[t], 0, f)),
                pl.BlockSpec((1, d, tf), lambda t, f, te, na, src: (te[t], 0, f)),
                pl.BlockSpec((1, tf, d), lambda t, f, te, na, src: (te[t], f, 0)),
            ],
            out_specs=pl.BlockSpec((tm, d), lambda t, f, te, na, src: (t, 0)),
            scratch_shapes=[pltpu.VMEM((2, tm, d), F32), pltpu.VMEM((tm, d), wdt), pltpu.VMEM((tm, d), F32),
                            pltpu.SemaphoreType.DMA((2,))],
        ),
        compiler_params=_params("arbitrary", "arbitrary"),
        name="moe_ffn",
    )(tile_expert, n_active, src_token, x, gain.reshape(1, d), wg.astype(wdt), wu.astype(wdt), wd.astype(wdt))


def _combine_kernel(slot_ref, x_ref, wts_ref, *refs, tc, has_norm):
    if has_norm:
        gain_ref, y_ref, o_ref, buf_ref, sem = refs
    else:
        y_ref, o_ref, buf_ref, sem = refs
    i = pl.program_id(0)
    slot = i % 2

    def start_gather(step, dst_slot):
        def issue(r, carry):
            for k in range(2):
                _row_copy(y_ref, buf_ref.at[dst_slot, k], sem.at[dst_slot], slot_ref[2 * (step * tc + r) + k], r).start()
            return carry

        lax.fori_loop(0, tc, issue, 0, unroll=_ISSUE_UNROLL)

    @pl.when(i == 0)
    def _():
        start_gather(0, 0)

    @pl.when(i + 1 < pl.num_programs(0))
    def _():
        start_gather(i + 1, 1 - slot)

    for k in range(2):
        pltpu.make_async_copy(y_ref.at[pl.ds(0, tc)], buf_ref.at[slot, k], sem.at[slot]).wait()
    w = wts_ref[...]
    out = x_ref[...] + (w[:, 0:1] * buf_ref[slot, 0] + w[:, 1:2] * buf_ref[slot, 1])
    if has_norm:
        out = _rms(out, gain_ref[...])
    o_ref[...] = out


def _combine(x, wts, y_sorted, slots, final_gain, *, tc=256):
    m, d = x.shape
    tc = min(tc, m)
    assert m % tc == 0
    has_norm = final_gain is not None
    args = [slots, x, wts]
    in_specs = [pl.BlockSpec((tc, d), lambda i, sl: (i, 0)), pl.BlockSpec((tc, LANES), lambda i, sl: (i, 0))]
    if has_norm:
        args.append(final_gain.reshape(1, d))
        in_specs.append(pl.BlockSpec((1, d), lambda i, sl: (0, 0)))
    args.append(y_sorted)
    in_specs.append(pl.BlockSpec(memory_space=pl.ANY))
    return pl.pallas_call(
        functools.partial(_combine_kernel, tc=tc, has_norm=has_norm),
        out_shape=jax.ShapeDtypeStruct((m, d), F32),
        grid_spec=pltpu.PrefetchScalarGridSpec(
            num_scalar_prefetch=1,
            grid=(m // tc,),
            in_specs=in_specs,
            out_specs=pl.BlockSpec((tc, d), lambda i, sl: (i, 0)),
            scratch_shapes=[pltpu.VMEM((2, 2, tc, d), F32), pltpu.SemaphoreType.DMA((2,))],
        ),
        compiler_params=_params("arbitrary"),
        name="moe_combine",
    )(*args)


def _moe_layer(x, gain, w_router, wg, wu, wd, final_gain, *, tm, tf, hp=False):
    m, d = x.shape
    n_e = wg.shape[0]
    idx, wts = _router(x, gain, w_router)
    e_flat = idx[:, :2].reshape(-1)
    onehot = (e_flat[:, None] == jnp.arange(n_e, dtype=jnp.int32)[None, :]).astype(jnp.int32)
    csum = jnp.cumsum(onehot, axis=0)
    rank = jnp.sum(csum * onehot, axis=1) - 1
    tiles_e = (csum[-1] + tm - 1) // tm
    tile_end = jnp.cumsum(tiles_e)
    slot = ((tile_end - tiles_e)[e_flat] * tm + rank).astype(jnp.int32)
    n_tiles = (2 * m) // tm + n_e
    tile_expert = jnp.minimum(
        jnp.sum(jnp.arange(n_tiles, dtype=jnp.int32)[:, None] >= tile_end[None, :], axis=1), n_e - 1).astype(jnp.int32)
    n_active = tile_end[-1:].astype(jnp.int32)
    src_token = jnp.zeros((n_tiles * tm,), jnp.int32).at[slot].set(jnp.arange(2 * m, dtype=jnp.int32) // 2)
    ys = _moe_ffn(x, src_token, gain, wg, wu, wd, tile_expert, n_active, tm=tm, tf=tf, hp=hp)
    return _combine(x, wts, ys, slot, final_gain)


def _gdn_weights(w_in, a_log, dt_bias):
    h_n = GDN_HEADS
    main = 2 * h_n * GDN_DK + 2 * h_n * GDN_DV
    w_gate = jnp.pad(w_in[:, main:], ((0, 0), (0, LANES - 2 * h_n)))
    pad = (h_n, LANES - 2 * h_n)
    alog_row = jnp.pad(a_log.astype(F32), pad).reshape(1, LANES)
    dtb_row = jnp.pad(dt_bias.astype(F32), pad).reshape(1, LANES)
    return w_in[:, :main], w_gate, alog_row, dtb_row


def _gdn_layer_prompt(x, gain, w_in, conv_w, a_log, dt_bias, o_gain, w_out, *, batch, seq):
    conv_dim = GDN_HEADS * (2 * GDN_DK + GDN_DV)
    w_main, w_gate, alog_row, dtb_row = _gdn_weights(w_in, a_log, dt_bias)
    proj = _matmul(x, w_main, gain=gain, tm=512, tn=2048, name="gdn_in_proj")
    gates = _matmul(x, w_gate, gain=gain, hp=True, tm=512, name="gdn_gate_proj")
    o, s_fin = _gdn_prompt(proj, gates, conv_w, alog_row, dtb_row, o_gain, batch=batch, seq=seq)
    y = _matmul(o, w_out, res=x, tm=512, name="gdn_out_proj")
    conv_state = proj.reshape(batch, seq, -1)[:, seq - (CONV_W - 1):, :conv_dim]
    return y, conv_state, s_fin


def _gdn_layer_step(x, gain, conv_prev, s_prev, w_in, conv_w, a_log, dt_bias, o_gain, w_out):
    w_main, w_gate, alog_row, dtb_row = _gdn_weights(w_in, a_log, dt_bias)
    proj = _matmul(x, w_main, gain=gain, hp=True, tn=1024, name="gdn_in_proj_s")
    gates = _matmul(x, w_gate, gain=gain, hp=True, name="gdn_gate_proj_s")
    o, conv_new, s_new = _gdn_step(proj, gates, conv_prev, s_prev, conv_w, alog_row, dtb_row, o_gain)
    y = _matmul(o.reshape(x.shape[0], -1), w_out, res=x, hp=True, name="gdn_out_proj_s")
    return y, conv_new, s_new


def _moba_layer_prompt(x, gain, w_in, w_out, *, batch, seq):
    proj = _matmul(x, w_in, gain=gain, tm=512, name="att_in_proj")
    cos_t, sin_a, sin_b = _rope_tables(jnp.arange(seq, dtype=jnp.int32))
    o, k_rot = _moba_prompt(proj, cos_t, sin_a, sin_b, batch=batch, seq=seq)
    y = _matmul(o, w_out, res=x, tm=512, name="att_out_proj")
    return y, k_rot, proj[:, (ATT_HEADS + KV_HEADS) * HEAD_DIM:]


def _moba_layer_step(x, gain, cache_k, cache_v, page_table, w_in, w_out):
    b = x.shape[0]
    hd = HEAD_DIM
    n_pool, page = cache_k.shape[0], cache_k.shape[1]
    past_len = page_table.shape[1] * page
    own = past_len // MOBA_BLOCK
    assert own * MOBA_BLOCK == past_len and own >= 1
    n_sel = min(MOBA_TOPK, own)
    ppb = MOBA_BLOCK // page
    proj = _matmul(x, w_in, gain=gain, hp=True, tn=1024, name="att_in_proj_s")
    ck = cache_k.reshape(n_pool, page * KV_HEADS, hd)
    cv = cache_v.reshape(n_pool, page * KV_HEADS, hd)
    kmeans = _cache_block_means(ck, page_table)
    cos_t, sin_a, sin_b = _rope_tables(jnp.full((1,), past_len, jnp.int32))
    q, k_rot, sel = _moba_select(proj, kmeans, cos_t, sin_a, sin_b, n_sel=n_sel)
    logical = sel[:, :, :n_sel, None] * ppb + jnp.arange(ppb, dtype=jnp.int32)
    pages = jnp.take_along_axis(page_table, logical.reshape(b, -1), axis=1).reshape(-1)
    v_new = proj[:, (ATT_HEADS + KV_HEADS) * hd:]
    o = _moba_step(q, k_rot.reshape(b, KV_HEADS, hd), v_new.reshape(b, KV_HEADS, hd), ck, cv, pages)
    y = _matmul(o, w_out, res=x, hp=True, name="att_out_proj_s")
    return y, k_rot.reshape(b, KV_HEADS * hd), v_new


def _norm_kernel(x_ref, g_ref, o_ref):
    o_ref[...] = _rms(x_ref[...], g_ref[...])


def _final_norm(x, gain, *, tm=512):
    m, d = x.shape
    tm = min(tm, m)
    return pl.pallas_call(
        _norm_kernel,
        out_shape=jax.ShapeDtypeStruct((m, d), F32),
        grid=(m // tm,),
        in_specs=[pl.BlockSpec((tm, d), lambda i: (i, 0)), pl.BlockSpec((1, d), lambda i: (0, 0))],
        out_specs=pl.BlockSpec((tm, d), lambda i: (i, 0)),
        compiler_params=_params("parallel"),
        name="final_norm",
    )(x, gain.reshape(1, d))


def kernel(x_prompt, x_sample, state_conv, state_delta, cache_k, cache_v, page_table, norm_mix, norm_ffn,
           norm_final, gdn_w_in, gdn_conv_w, gdn_a_log, gdn_dt_bias, gdn_o_norm, gdn_w_out, att_w_in,
           att_w_out, ffn_w_gate, ffn_w_up, ffn_w_down, moe_router, moe_w_gate, moe_w_up, moe_w_down):
    bp, seq, d = x_prompt.shape
    db, dec_seq, _ = x_sample.shape
    assert dec_seq == 1
    depth = norm_mix.shape[0]
    hp = x_prompt.reshape(bp * seq, d)
    hs = x_sample.reshape(db, d)
    conv_p, delta_p, k_p, v_p = [], [], [], []
    conv_s, delta_s, k_s, v_s = [], [], [], []
    for i in range(depth):
        j = i // 2
        last = i == depth - 1
        if i % 2 == 0:
            gdn = (gdn_w_in[j], gdn_conv_w[j], gdn_a_log[j], gdn_dt_bias[j], gdn_o_norm[j], gdn_w_out[j])
            hp, cp, sp = _gdn_layer_prompt(hp, norm_mix[i], *gdn, batch=bp, seq=seq)
            hs, cs, ss = _gdn_layer_step(hs, norm_mix[i], state_conv[j], state_delta[j], *gdn)
            conv_p.append(cp)
            delta_p.append(sp.astype(state_delta.dtype))
            conv_s.append(cs)
            delta_s.append(ss.astype(state_delta.dtype))
            ffn = (norm_ffn[i], ffn_w_gate[j], ffn_w_up[j], ffn_w_down[j])
            hp = _ffn(hp, *ffn)
            hs = _ffn(hs, *ffn, hp=True)
            if last:
                hp = _final_norm(hp, norm_final)
                hs = _final_norm(hs, norm_final)
        else:
            hp, kp, vp = _moba_layer_prompt(hp, norm_mix[i], att_w_in[j], att_w_out[j], batch=bp, seq=seq)
            hs, kn, vn = _moba_layer_step(hs, norm_mix[i], cache_k[j], cache_v[j], page_table,
                                          att_w_in[j], att_w_out[j])
            k_p.append(kp.reshape(bp, seq, KV_HEADS, HEAD_DIM))
            v_p.append(vp.reshape(bp, seq, KV_HEADS, HEAD_DIM))
            k_s.append(kn.reshape(db, 1, KV_HEADS, HEAD_DIM))
            v_s.append(vn.reshape(db, 1, KV_HEADS, HEAD_DIM))
            moe = (norm_ffn[i], moe_router[j], moe_w_gate[j], moe_w_up[j], moe_w_down[j])
            fg = norm_final if last else None
            hp = _moe_layer(hp, *moe, fg, tm=512, tf=1792)
            hs = _moe_layer(hs, *moe, fg, tm=32, tf=512, hp=True)
    return (hp.reshape(bp, seq, d), hs.reshape(db, 1, d),
            jnp.stack(conv_p), jnp.stack(delta_p), jnp.stack(k_p), jnp.stack(v_p),
            jnp.stack(conv_s), jnp.stack(delta_s), jnp.stack(k_s), jnp.stack(v_s))
```

```python
import functools
import math

import jax
import jax.numpy as jnp
from jax import lax
from jax.experimental import pallas as pl
from jax.experimental.pallas import tpu as pltpu

F32 = jnp.float32
BF16 = jnp.bfloat16

GDN_HEADS = 8
GDN_DK = 128
GDN_DV = 128
CONV_W = 4
GDN_CHUNK = 64
ATT_HEADS = 8
KV_HEADS = 4
HEAD_DIM = 128
ROT_DIM = HEAD_DIM // 4
ROPE_THETA = 500000.0
MOBA_BLOCK = 256
MOBA_TOPK = 3
N_EXPERTS = 8
EPS = 1e-6

LANES = 128
SUBLANES = 8
VMEM_LIMIT = 56 * 1024 * 1024

NEG_INF = float("-inf")
_ISSUE_UNROLL = 8


def _params(*sem):
    return pltpu.CompilerParams(dimension_semantics=sem, vmem_limit_bytes=VMEM_LIMIT)


def _rms(x, gain):
    return x * lax.rsqrt(jnp.mean(x * x, axis=-1, keepdims=True) + EPS) * gain


def _silu(x):
    return x * jax.nn.sigmoid(x)


def _softplus(x):
    return jnp.maximum(x, 0.0) + jnp.log1p(jnp.exp(-jnp.abs(x)))


def _split2(a):
    hi = a.astype(BF16)
    lo = (a - hi.astype(F32)).astype(BF16)
    return hi, lo


def _split3(a):
    hi = a.astype(BF16)
    r = a - hi.astype(F32)
    mid = r.astype(BF16)
    lo = (r - mid.astype(F32)).astype(BF16)
    return hi, mid, lo


_NN = (((1,), (0,)), ((), ()))
_NT = (((1,), (1,)), ((), ()))
_TN = (((0,), (0,)), ((), ()))


def _dot(a, b, dims=_NN):
    return lax.dot_general(a, b, dims, preferred_element_type=F32)


def _dot_hp(a, b, dims=_NN):
    ah, al = _split2(a)
    bh, bl = _split2(b)
    return _dot(ah, bh, dims) + (_dot(al, bh, dims) + _dot(ah, bl, dims))


def _mm_kernel(*refs, has_gain, has_res, hp):
    it = iter(refs)
    a_ref = next(it)
    g_ref = next(it) if has_gain else None
    w_ref = next(it)
    r_ref = next(it) if has_res else None
    o_ref = next(it)
    an_ref = next(it)

    @pl.when(pl.program_id(1) == 0)
    def _():
        a = a_ref[...].astype(F32)
        if has_gain:
            a = _rms(a, g_ref[...])
        an_ref[...] = a.astype(an_ref.dtype)

    acc = (_dot_hp if hp else _dot)(an_ref[...], w_ref[...])
    if has_res:
        acc = acc + r_ref[...]
    o_ref[...] = acc.astype(o_ref.dtype)


def _matmul(a, w, *, gain=None, res=None, hp=False, tm=512, tn=None, out_dtype=F32, name="matmul"):
    m, k = a.shape
    n = w.shape[1]
    tm = min(tm, m)
    tn = n if tn is None else min(tn, n)
    assert m % tm == 0 and n % tn == 0
    args = [a]
    in_specs = [pl.BlockSpec((tm, k), lambda i, j: (i, 0))]
    if gain is not None:
        args.append(gain.reshape(1, k).astype(F32))
        in_specs.append(pl.BlockSpec((1, k), lambda i, j: (0, 0)))
    args.append(w if hp else w.astype(BF16))
    in_specs.append(pl.BlockSpec((k, tn), lambda i, j: (0, j)))
    if res is not None:
        args.append(res)
        in_specs.append(pl.BlockSpec((tm, tn), lambda i, j: (i, j)))
    scratch = [pltpu.VMEM((tm, k), F32 if hp else BF16)]
    return pl.pallas_call(
        functools.partial(_mm_kernel, has_gain=gain is not None, has_res=res is not None, hp=hp),
        out_shape=jax.ShapeDtypeStruct((m, n), out_dtype),
        grid=(m // tm, n // tn),
        in_specs=in_specs,
        out_specs=pl.BlockSpec((tm, tn), lambda i, j: (i, j)),
        scratch_shapes=scratch,
        compiler_params=_params("parallel", "arbitrary"),
        name=name,
    )(*args)


def _swiglu_step(xn, wg, wu, wd, hp):
    dot = _dot_hp if hp else _dot
    act = _silu(dot(xn, wg)) * dot(xn, wu)
    return dot(act if hp else act.astype(BF16), wd)


def _ffn_kernel(x_ref, g_ref, wg_ref, wu_ref, wd_ref, o_ref, xn_ref, acc_ref, *, hp):
    f = pl.program_id(1)

    @pl.when(f == 0)
    def _():
        xn_ref[...] = _rms(x_ref[...], g_ref[...]).astype(xn_ref.dtype)
        acc_ref[...] = jnp.zeros_like(acc_ref)

    acc_ref[...] += _swiglu_step(xn_ref[...], wg_ref[...], wu_ref[...], wd_ref[...], hp)

    @pl.when(f == pl.num_programs(1) - 1)
    def _():
        o_ref[...] = x_ref[...] + acc_ref[...]


def _ffn(x, gain, wg, wu, wd, *, hp=False, tm=1024, tf=512):
    m, d = x.shape
    dff = wg.shape[1]
    tm = min(tm, m)
    assert m % tm == 0 and dff % tf == 0
    wdt = F32 if hp else BF16
    return pl.pallas_call(
        functools.partial(_ffn_kernel, hp=hp),
        out_shape=jax.ShapeDtypeStruct((m, d), F32),
        grid=(m // tm, dff // tf),
        in_specs=[
            pl.BlockSpec((tm, d), lambda i, f: (i, 0)),
            pl.BlockSpec((1, d), lambda i, f: (0, 0)),
            pl.BlockSpec((d, tf), lambda i, f: (0, f)),
            pl.BlockSpec((d, tf), lambda i, f: (0, f)),
            pl.BlockSpec((tf, d), lambda i, f: (f, 0)),
        ],
        out_specs=pl.BlockSpec((tm, d), lambda i, f: (i, 0)),
        scratch_shapes=[pltpu.VMEM((tm, d), wdt), pltpu.VMEM((tm, d), F32)],
        compiler_params=_params("parallel", "arbitrary"),
        name="ffn",
    )(x, gain.reshape(1, d), wg.astype(wdt), wu.astype(wdt), wd.astype(wdt))


def _bdot(a, b, dims=_NN):
    return _dot(a.astype(BF16), b.astype(BF16), dims)


def _neumann_inverse_minus_eye(mats):
    c = mats[0].shape[0]
    ps = [-a for a in mats]
    ns = list(ps)
    for _ in range(int(math.log2(c)) - 1):
        ps = [_bdot(p, p) for p in ps]
        ns = [n + p + _bdot(n, p) for n, p in zip(ns, ps)]
    return ns


def _gdn_kernel(proj_ref, gate_ref, convw_ref, alog_ref, dtb_ref, ogain_ref,
                o_ref, sfin_ref, s_ref, tail_ref, *, tt, c):
    h_n, dk, dv = GDN_HEADS, GDN_DK, GDN_DV
    t = pl.program_id(1)

    @pl.when(t == 0)
    def _():
        s_ref[...] = jnp.zeros_like(s_ref)
        tail_ref[...] = jnp.zeros_like(tail_ref)

    gt = gate_ref[...]
    beta_all = jax.nn.sigmoid(gt)
    g_all = -jnp.exp(alog_ref[...]) * _softplus(gt + dtb_ref[...])

    ri = lax.broadcasted_iota(jnp.int32, (c, c), 0)
    ci = lax.broadcasted_iota(jnp.int32, (c, c), 1)
    tri = ri >= ci
    strict = ri > ci
    tri_b = tri.astype(BF16)
    ri2 = lax.broadcasted_iota(jnp.int32, (c, LANES), 0)
    ci2 = lax.broadcasted_iota(jnp.int32, (c, LANES), 1)
    upper_b = (ri2 <= ci2).astype(BF16)

    n_chunks = tt // c
    gams, gam_ts = [], []
    for ck in range(n_chunks):
        parts = _split3(g_all[ck * c:(ck + 1) * c, :])
        gam = sum(_dot(tri_b, p) for p in parts)
        gam_t = sum(_dot(p, upper_b, _TN) for p in parts)
        gams.append(gam)
        gam_ts.append(gam_t)

    def conv_silu(off):
        raw = proj_ref[:, off:off + LANES]
        xs = jnp.concatenate([tail_ref[:, off:off + LANES], raw], axis=0)
        w = convw_ref[:, off:off + LANES]
        acc = raw * w[CONV_W - 1:CONV_W]
        for j in range(CONV_W - 1):
            lo = SUBLANES - (CONV_W - 1) + j
            acc = acc + xs[lo:lo + tt] * w[j:j + 1]
        return _silu(acc)

    def l2n(x):
        return x * lax.rsqrt(jnp.sum(x * x, axis=-1, keepdims=True) + 1e-6)

    heads = range(h_n)
    pairs = [(ck, h) for ck in range(n_chunks) for h in heads]
    z_off = 2 * h_n * dk + h_n * dv
    q_all = [l2n(conv_silu(h * dk)) * (dk ** -0.5) for h in heads]
    k_all = [l2n(conv_silu(h_n * dk + h * dk)) for h in heads]
    v_all = [conv_silu(2 * h_n * dk + h * dv) for h in heads]

    def rows(x, ck):
        return x[ck * c:(ck + 1) * c]

    qc = [rows(q_all[h], ck) for ck, h in pairs]
    kc = [rows(k_all[h], ck) for ck, h in pairs]
    bcol = [rows(beta_all, ck)[:, h:h + 1] for ck, h in pairs]
    col = [gams[ck][:, h_n + h:h_n + h + 1] for ck, h in pairs]
    dm = [jnp.exp(jnp.where(tri, cl - gam_ts[ck][h_n + h:h_n + h + 1, :c], NEG_INF))
          for cl, (ck, h) in zip(col, pairs)]
    kb = [x * b for x, b in zip(kc, bcol)]
    kq = [_bdot(jnp.concatenate([x, y], axis=0), z, _NT) for x, y, z in zip(kb, qc, kc)]
    a = [jnp.where(strict, x[:c] * d, 0.0) for x, d in zip(kq, dm)]
    attn = [(x[c:] * d).astype(BF16) for x, d in zip(kq, dm)]
    ninv = _neumann_inverse_minus_eye(a)
    eg = [jnp.exp(cl) for cl in col]
    rhs = [jnp.concatenate([rows(v_all[h], ck) * b, x * e], axis=1)
           for (ck, h), b, x, e in zip(pairs, bcol, kb, eg)]
    sol = [r + _bdot(n, r) for n, r in zip(ninv, rhs)]
    glast = [cl[c - 1:c, :] for cl in col]
    lhs = [jnp.concatenate([x[:, dv:], y * e], axis=0).astype(BF16)
           for x, y, e in zip(sol, qc, eg)]
    k_dec = [(x * jnp.exp(g - cl)).astype(BF16) for x, g, cl in zip(kc, glast, col)]
    g_end = [jnp.exp(g) for g in glast]

    ogain = ogain_ref[...]
    for ck in range(n_chunks):
        idx = [ck * h_n + h for h in heads]
        s_old = [s_ref[h] for h in heads]
        both = [_dot(lhs[i], s.astype(BF16)) for i, s in zip(idx, s_old)]
        ub = [(sol[i][:, :dv] - b[:c]).astype(BF16) for i, b in zip(idx, both)]
        o = [b[c:] + _dot(attn[i], u) for i, b, u in zip(idx, both, ub)]
        for h, i, s, u in zip(heads, idx, s_old, ub):
            s_ref[h] = s * g_end[i] + _dot(k_dec[i], u, _TN)
        for h, x in zip(heads, o):
            zc = proj_ref[ck * c:(ck + 1) * c, z_off + h * dv:z_off + (h + 1) * dv]
            o_ref[ck * c:(ck + 1) * c, h * dv:(h + 1) * dv] = (_rms(x, ogain) * _silu(zc)).astype(o_ref.dtype)

    tail_ref[...] = proj_ref[tt - SUBLANES:tt, :tail_ref.shape[1]]

    @pl.when(t == pl.num_programs(1) - 1)
    def _():
        sfin_ref[0] = s_ref[...]


def _gdn_prompt(proj, gates, conv_w, alog_row, dtb_row, o_gain, *, batch, seq, tt=256):
    h_n, dk, dv = GDN_HEADS, GDN_DK, GDN_DV
    conv_dim = h_n * (2 * dk + dv)
    width = proj.shape[1]
    tt = min(tt, seq)
    c = math.gcd(seq, GDN_CHUNK)
    assert seq % tt == 0 and tt % c == 0 and tt >= SUBLANES
    nt = seq // tt
    return pl.pallas_call(
        functools.partial(_gdn_kernel, tt=tt, c=c),
        out_shape=(jax.ShapeDtypeStruct((batch * seq, h_n * dv), BF16),
                   jax.ShapeDtypeStruct((batch, h_n, dk, dv), F32)),
        grid=(batch, nt),
        in_specs=[
            pl.BlockSpec((tt, width), lambda b, t: (b * nt + t, 0)),
            pl.BlockSpec((tt, LANES), lambda b, t: (b * nt + t, 0)),
            pl.BlockSpec((CONV_W, conv_dim), lambda b, t: (0, 0)),
            pl.BlockSpec((1, LANES), lambda b, t: (0, 0)),
            pl.BlockSpec((1, LANES), lambda b, t: (0, 0)),
            pl.BlockSpec((1, dv), lambda b, t: (0, 0)),
        ],
        out_specs=(pl.BlockSpec((tt, h_n * dv), lambda b, t: (b * nt + t, 0)),
                   pl.BlockSpec((1, h_n, dk, dv), lambda b, t: (b, 0, 0, 0))),
        scratch_shapes=[pltpu.VMEM((h_n, dk, dv), F32), pltpu.VMEM((SUBLANES, conv_dim), F32)],
        compiler_params=_params("parallel", "arbitrary"),
        name="gdn_prompt",
    )(proj, gates, conv_w, alog_row, dtb_row, o_gain.reshape(1, dv))


def _gdn_step_kernel(proj_ref, gate_ref, cprev_ref, s_ref, convw_ref, alog_ref, dtb_ref, ogain_ref,
                     o_ref, cnew_ref, snew_ref):
    h_n, dk, dv = GDN_HEADS, GDN_DK, GDN_DV
    conv_dim = h_n * (2 * dk + dv)
    u_row = proj_ref[0]
    prev = cprev_ref[0]
    w = convw_ref[...]
    raw = u_row[:, :conv_dim]
    acc = raw * w[CONV_W - 1:CONV_W]
    for j in range(CONV_W - 1):
        acc = acc + prev[j:j + 1] * w[j:j + 1]
    qkv = _silu(acc)
    for j in range(CONV_W - 2):
        cnew_ref[0, j:j + 1, :] = prev[j + 1:j + 2]
    cnew_ref[0, CONV_W - 2:CONV_W - 1, :] = raw

    gt = gate_ref[0]
    beta_all = jax.nn.sigmoid(gt)
    g_all = -jnp.exp(alog_ref[...]) * _softplus(gt + dtb_ref[...])
    rows = lax.broadcasted_iota(jnp.int32, (SUBLANES, LANES), 0)
    ogain = ogain_ref[...]

    def l2n(x):
        return x * lax.rsqrt(jnp.sum(x * x, axis=-1, keepdims=True) + 1e-6)

    def as_row0(x, y=None):
        out = jnp.where(rows == 0, jnp.broadcast_to(x, (SUBLANES, LANES)), 0.0)
        if y is not None:
            out = jnp.where(rows == 1, jnp.broadcast_to(y, (SUBLANES, LANES)), out)
        return out

    for h in range(h_n):
        q = l2n(qkv[:, h * dk:(h + 1) * dk]) * (dk ** -0.5)
        k = l2n(qkv[:, h_n * dk + h * dk:h_n * dk + (h + 1) * dk])
        v = qkv[:, 2 * h_n * dk + h * dv:2 * h_n * dk + (h + 1) * dv]
        z = u_row[:, conv_dim + h * dv:conv_dim + (h + 1) * dv]
        beta = beta_all[:, h:h + 1]
        eg = jnp.exp(g_all[:, h_n + h:h_n + h + 1])
        s = s_ref[0, h]
        kcd = k * (beta * eg)
        q_dec = q * eg
        both = _dot_hp(as_row0(kcd, q_dec), s)
        u = v * beta - both[0:1]
        o = both[1:2] + jnp.sum(q * k, axis=-1, keepdims=True) * u
        snew_ref[0, h] = s * eg + _dot_hp(as_row0(k), as_row0(u), _TN)
        o_ref[0, :, h * dv:(h + 1) * dv] = (_rms(o, ogain) * _silu(z)).astype(o_ref.dtype)


def _gdn_step(proj, gates, conv_prev, s_prev, conv_w, alog_row, dtb_row, o_gain):
    h_n, dk, dv = GDN_HEADS, GDN_DK, GDN_DV
    conv_dim = h_n * (2 * dk + dv)
    b, width = proj.shape
    return pl.pallas_call(
        _gdn_step_kernel,
        out_shape=(jax.ShapeDtypeStruct((b, 1, h_n * dv), F32),
                   jax.ShapeDtypeStruct((b, CONV_W - 1, conv_dim), F32),
                   jax.ShapeDtypeStruct((b, h_n, dk, dv), F32)),
        grid=(b,),
        in_specs=[
            pl.BlockSpec((1, 1, width), lambda i: (i, 0, 0)),
            pl.BlockSpec((1, 1, LANES), lambda i: (i, 0, 0)),
            pl.BlockSpec((1, CONV_W - 1, conv_dim), lambda i: (i, 0, 0)),
            pl.BlockSpec((1, h_n, dk, dv), lambda i: (i, 0, 0, 0)),
            pl.BlockSpec((CONV_W, conv_dim), lambda i: (0, 0)),
            pl.BlockSpec((1, LANES), lambda i: (0, 0)),
            pl.BlockSpec((1, LANES), lambda i: (0, 0)),
            pl.BlockSpec((1, dv), lambda i: (0, 0)),
        ],
        out_specs=(pl.BlockSpec((1, 1, h_n * dv), lambda i: (i, 0, 0)),
                   pl.BlockSpec((1, CONV_W - 1, conv_dim), lambda i: (i, 0, 0)),
                   pl.BlockSpec((1, h_n, dk, dv), lambda i: (i, 0, 0, 0))),
        compiler_params=_params("parallel"),
        name="gdn_step",
    )(proj.reshape(b, 1, width), gates.reshape(b, 1, LANES), conv_prev, s_prev, conv_w,
      alog_row, dtb_row, o_gain.reshape(1, dv))


def _rope(x, cos, sin_a, sin_b):
    half = ROT_DIM // 2
    return (x * cos + pltpu.roll(x, HEAD_DIM - half, 1) * sin_a + pltpu.roll(x, half, 1) * sin_b)


def _rope_tables(pos):
    half = ROT_DIM // 2
    inv = jnp.power(ROPE_THETA, -jnp.arange(half, dtype=F32) * (2.0 / ROT_DIM))
    ang = pos.astype(F32)[:, None] * inv[None, :]
    cos, sin = jnp.cos(ang), jnp.sin(ang)
    n = pos.shape[0]
    rest = HEAD_DIM - ROT_DIM
    cos_t = jnp.concatenate([cos, cos, jnp.ones((n, rest), F32)], axis=1)
    sin_a = jnp.concatenate([-sin, jnp.zeros((n, half + rest), F32)], axis=1)
    sin_b = jnp.concatenate([jnp.zeros((n, half), F32), sin, jnp.zeros((n, rest), F32)], axis=1)
    return cos_t, sin_a, sin_b


def _moba_kernel(q_ref, k_ref, v_ref, cos_ref, sa_ref, sb_ref, o_ref, kf_ref, vf_ref,
                 kout_ref, kb_ref, vb_ref, *, seq):
    blk, hd = MOBA_BLOCK, HEAD_DIM
    group = ATT_HEADS // KV_HEADS
    nb = seq // blk
    nq = group * blk
    scale = hd ** -0.5

    kout_ref[...] = _rope(k_ref[...], cos_ref[...], sa_ref[...], sb_ref[...])
    kb_ref[...] = kout_ref[...].astype(BF16)
    vb_ref[...] = v_ref[...].astype(BF16)
    for gg in range(KV_HEADS):
        @pl.when(pl.program_id(1) == gg)
        def _(gg=gg):
            kf_ref[pl.ds(gg, seq, stride=KV_HEADS), :] = kout_ref[...]
            vf_ref[pl.ds(gg, seq, stride=KV_HEADS), :] = v_ref[...]

    kmean = jnp.concatenate(
        [jnp.sum(kout_ref[n * blk:(n + 1) * blk, :], axis=0, keepdims=True) for n in range(nb)],
        axis=0) * (1.0 / blk)

    blk_id = lax.broadcasted_iota(jnp.int32, (nb, nq), 0)
    key_r = lax.broadcasted_iota(jnp.int32, (blk, nq), 0)
    qry_c = lax.broadcasted_iota(jnp.int32, (blk, nq), 1) % blk
    causal = key_r <= qry_c

    for i in range(nb):
        rows = slice(i * blk, (i + 1) * blk)
        cos, sa, sb = cos_ref[rows, :], sa_ref[rows, :], sb_ref[rows, :]
        qs = [_rope(q_ref[rows, g * hd:(g + 1) * hd], cos, sa, sb) for g in range(group)]
        q = jnp.concatenate(qs, axis=0)
        qb = q.astype(BF16)

        if i > MOBA_TOPK:
            gate = jnp.where(blk_id < i, _dot_hp(kmean, q, _NT), NEG_INF)
            rank = jnp.zeros((nb, nq), F32)
            for m in range(i):
                gm = gate[m:m + 1, :]
                rank = rank + jnp.where((gm > gate) | ((gm == gate) & (m < blk_id)), 1.0, 0.0)
            sel = jnp.where((blk_id < i) & (rank < MOBA_TOPK), 1.0, 0.0)
        else:
            sel = None

        ss = []
        for j in range(i + 1):
            s = _dot(kb_ref[j * blk:(j + 1) * blk, :], qb, _NT) * scale
            if j == i:
                s = jnp.where(causal, s, NEG_INF)
            elif sel is not None:
                s = jnp.where(sel[j:j + 1, :] > 0.0, s, NEG_INF)
            ss.append(s)
        m_all = functools.reduce(jnp.maximum, [jnp.max(s, axis=0, keepdims=True) for s in ss])
        ps = [jnp.exp(s - m_all) for s in ss]
        l_all = functools.reduce(jnp.add, [jnp.sum(p, axis=0, keepdims=True) for p in ps])
        acc = functools.reduce(
            jnp.add, [_dot(vb_ref[j * blk:(j + 1) * blk, :], p.astype(BF16), _TN) for j, p in enumerate(ps)])
        out = (acc / l_all).T
        for g in range(group):
            o_ref[rows, g * hd:(g + 1) * hd] = out[g * blk:(g + 1) * blk].astype(o_ref.dtype)


def _moba_prompt(proj, cos_t, sin_a, sin_b, *, batch, seq):
    hd = HEAD_DIM
    group = ATT_HEADS // KV_HEADS
    assert seq % MOBA_BLOCK == 0
    tbl = pl.BlockSpec((seq, hd), lambda b, g: (0, 0))
    kv_rows = jax.ShapeDtypeStruct((batch * seq * KV_HEADS, hd), F32)
    kv_spec = pl.BlockSpec((seq * KV_HEADS, hd), lambda b, g: (b, 0))
    return pl.pallas_call(
        functools.partial(_moba_kernel, seq=seq),
        out_shape=(jax.ShapeDtypeStruct((batch * seq, ATT_HEADS * hd), BF16), kv_rows, kv_rows),
        grid=(batch, KV_HEADS),
        in_specs=[
            pl.BlockSpec((seq, group * hd), lambda b, g: (b, g)),
            pl.BlockSpec((seq, hd), lambda b, g: (b, ATT_HEADS + g)),
            pl.BlockSpec((seq, hd), lambda b, g: (b, ATT_HEADS + KV_HEADS + g)),
            tbl, tbl, tbl,
        ],
        out_specs=(pl.BlockSpec((seq, group * hd), lambda b, g: (b, g)), kv_spec, kv_spec),
        scratch_shapes=[pltpu.VMEM((seq, hd), F32), pltpu.VMEM((seq, hd), BF16), pltpu.VMEM((seq, hd), BF16)],
        compiler_params=_params("parallel", "arbitrary"),
        name="moba_prompt",
    )(proj, proj, proj, cos_t, sin_a, sin_b)


def _kmean_kernel(pt_ref, *refs, pages, pages_per_block):
    page_refs, o_ref = refs[:pages], refs[pages]
    rows = page_refs[0].shape[1]
    tokens = pages_per_block * rows // KV_HEADS
    for n in range(pages // pages_per_block):
        part = jnp.zeros((SUBLANES, HEAD_DIM), F32)
        for p in range(pages_per_block):
            x = page_refs[n * pages_per_block + p][0]
            part = part + jnp.sum(x.reshape(rows // SUBLANES, SUBLANES, HEAD_DIM), axis=0)
        acc = part[:KV_HEADS]
        for r in range(1, SUBLANES // KV_HEADS):
            acc = acc + part[r * KV_HEADS:(r + 1) * KV_HEADS]
        o_ref[0, n] = acc * (1.0 / tokens)


def _cache_block_means(cache, page_table, *, blocks_per_step=8):
    n_pool, rows, hd = cache.shape
    page = rows // KV_HEADS
    b, n_pages = page_table.shape
    ppb = MOBA_BLOCK // page
    assert ppb * page == MOBA_BLOCK and n_pages % ppb == 0 and rows % SUBLANES == 0
    n_blocks = n_pages // ppb
    bps = math.gcd(blocks_per_step, n_blocks)
    pages = bps * ppb
    specs = [pl.BlockSpec((1, rows, hd), lambda i, s, pt, p=p: (pt[i, s * pages + p], 0, 0))
             for p in range(pages)]
    return pl.pallas_call(
        functools.partial(_kmean_kernel, pages=pages, pages_per_block=ppb),
        out_shape=jax.ShapeDtypeStruct((b, n_blocks, KV_HEADS, hd), F32),
        grid_spec=pltpu.PrefetchScalarGridSpec(
            num_scalar_prefetch=1,
            grid=(b, n_blocks // bps),
            in_specs=specs,
            out_specs=pl.BlockSpec((1, bps, KV_HEADS, hd), lambda i, s, pt: (i, s, 0, 0)),
        ),
        compiler_params=_params("parallel", "arbitrary"),
        name="cache_block_means",
    )(page_table, *([cache] * pages))


def _moba_select_kernel(proj_ref, km_ref, cos_ref, sa_ref, sb_ref, q_ref, k_ref, sel_ref, *, n_sel):
    hd = HEAD_DIM
    group = ATT_HEADS // KV_HEADS
    row = proj_ref[0]
    cos, sa, sb = cos_ref[...], sa_ref[...], sb_ref[...]
    rows = lax.broadcasted_iota(jnp.int32, (SUBLANES, hd), 0)
    q = jnp.zeros((SUBLANES, hd), F32)
    for h in range(ATT_HEADS):
        qh = _rope(row[:, h * hd:(h + 1) * hd], cos, sa, sb)
        q = jnp.where(rows == h, jnp.broadcast_to(qh, (SUBLANES, hd)), q)
    q_ref[0] = q
    for g in range(KV_HEADS):
        off = (ATT_HEADS + g) * hd
        k_ref[0, :, g * hd:(g + 1) * hd] = _rope(row[:, off:off + hd], cos, sa, sb)

    nb = km_ref.shape[1]
    hrow = lax.broadcasted_iota(jnp.int32, (ATT_HEADS, nb), 0)
    lane = lax.broadcasted_iota(jnp.int32, (ATT_HEADS, nb), 1)
    gate = jnp.zeros((ATT_HEADS, nb), F32)
    for g in range(KV_HEADS):
        gg = _dot_hp(q, km_ref[0, :, g, :], _NT)
        gate = jnp.where(hrow // group == g, gg, gate)
    out_lane = lax.broadcasted_iota(jnp.int32, (ATT_HEADS, LANES), 1)
    sel = jnp.zeros((ATT_HEADS, LANES), jnp.int32)
    for t in range(n_sel):
        m = jnp.max(gate, axis=-1, keepdims=True)
        idx = jnp.min(jnp.where(gate == m, lane, nb), axis=-1, keepdims=True)
        sel = jnp.where(out_lane == t, idx, sel)
        gate = jnp.where(lane == idx, NEG_INF, gate)
    sel_ref[0] = sel


def _moba_select(proj, kmeans, cos_t, sin_a, sin_b, *, n_sel):
    b, width = proj.shape
    nb = kmeans.shape[1]
    hd = HEAD_DIM
    tbl = pl.BlockSpec((1, hd), lambda i: (0, 0))
    return pl.pallas_call(
        functools.partial(_moba_select_kernel, n_sel=n_sel),
        out_shape=(jax.ShapeDtypeStruct((b, ATT_HEADS, hd), F32),
                   jax.ShapeDtypeStruct((b, 1, KV_HEADS * hd), F32),
                   jax.ShapeDtypeStruct((b, ATT_HEADS, LANES), jnp.int32)),
        grid=(b,),
        in_specs=[pl.BlockSpec((1, 1, width), lambda i: (i, 0, 0)),
                  pl.BlockSpec((1, nb, KV_HEADS, hd), lambda i: (i, 0, 0, 0)),
                  tbl, tbl, tbl],
        out_specs=(pl.BlockSpec((1, ATT_HEADS, hd), lambda i: (i, 0, 0)),
                   pl.BlockSpec((1, 1, KV_HEADS * hd), lambda i: (i, 0, 0)),
                   pl.BlockSpec((1, ATT_HEADS, LANES), lambda i: (i, 0, 0))),
        compiler_params=_params("parallel"),
        name="moba_select",
    )(proj.reshape(b, 1, width), kmeans, cos_t, sin_a, sin_b)


def _moba_step_kernel(pg_ref, q_ref, kn_ref, vn_ref, *refs, n_pages):
    k_refs, v_refs, o_ref = refs[:n_pages], refs[n_pages:2 * n_pages], refs[2 * n_pages]
    hd = HEAD_DIM
    group = ATT_HEADS // KV_HEADS
    scale = hd ** -0.5
    h = pl.program_id(1)
    g = h // group
    page = k_refs[0].shape[1] // KV_HEADS

    def head_rows(ref):
        out = ref[0, pl.ds(0, page, stride=KV_HEADS), :]
        for gg in range(1, KV_HEADS):
            out = jnp.where(g == gg, ref[0, pl.ds(gg, page, stride=KV_HEADS), :], out)
        return out

    q = q_ref[0, pl.ds(h, 1), :]
    q8 = jnp.broadcast_to(q, (SUBLANES, hd))
    kn = kn_ref[0, pl.ds(g, 1), :]
    vn = vn_ref[0, pl.ds(g, 1), :]
    s_self = jnp.sum(q * kn, axis=-1, keepdims=True) * scale
    scores = [_dot_hp(q8, head_rows(k_refs[p]), _NT)[0:1] * scale for p in range(n_pages)]
    m = functools.reduce(jnp.maximum, [jnp.max(s, axis=-1, keepdims=True) for s in scores], s_self)
    p_self = jnp.exp(s_self - m)
    l = p_self
    acc = p_self * vn
    for p in range(n_pages):
        e = jnp.exp(scores[p] - m)
        l = l + jnp.sum(e, axis=-1, keepdims=True)
        acc = acc + _dot_hp(jnp.broadcast_to(e, (SUBLANES, page)), head_rows(v_refs[p]))[0:1]
    o_ref[0] = (acc / l).astype(o_ref.dtype)


def _moba_step(q, k_new, v_new, cache_k, cache_v, pages):
    b = q.shape[0]
    hd = HEAD_DIM
    rows = cache_k.shape[1]
    n_pages = pages.shape[0] // (b * ATT_HEADS)

    def page_spec(p):
        return pl.BlockSpec((1, rows, hd), lambda i, h, pg, p=p: (pg[(i * ATT_HEADS + h) * n_pages + p], 0, 0))

    specs = ([pl.BlockSpec((1, ATT_HEADS, hd), lambda i, h, pg: (i, 0, 0)),
              pl.BlockSpec((1, KV_HEADS, hd), lambda i, h, pg: (i, 0, 0)),
              pl.BlockSpec((1, KV_HEADS, hd), lambda i, h, pg: (i, 0, 0))]
             + [page_spec(p) for p in range(n_pages)] * 2)
    out = pl.pallas_call(
        functools.partial(_moba_step_kernel, n_pages=n_pages),
        out_shape=jax.ShapeDtypeStruct((b * ATT_HEADS, 1, hd), F32),
        grid_spec=pltpu.PrefetchScalarGridSpec(
            num_scalar_prefetch=1,
            grid=(b, ATT_HEADS),
            in_specs=specs,
            out_specs=pl.BlockSpec((1, 1, hd), lambda i, h, pg: (i * ATT_HEADS + h, 0, 0)),
        ),
        compiler_params=_params("parallel", "arbitrary"),
        name="moba_step",
    )(pages, q, k_new, v_new, *([cache_k] * n_pages), *([cache_v] * n_pages))
    return out.reshape(b, ATT_HEADS * hd)


def _router_kernel(x_ref, g_ref, w_ref, idx_ref, wts_ref):
    logits = _dot_hp(_rms(x_ref[...], g_ref[...]), w_ref[...])
    lane = lax.broadcasted_iota(jnp.int32, logits.shape, 1)
    logits = jnp.where(lane < N_EXPERTS, logits, NEG_INF)
    m1 = jnp.max(logits, axis=-1, keepdims=True)
    i1 = jnp.min(jnp.where(logits == m1, lane, LANES), axis=-1, keepdims=True)
    rest = jnp.where(lane == i1, NEG_INF, logits)
    m2 = jnp.max(rest, axis=-1, keepdims=True)
    i2 = jnp.min(jnp.where(rest == m2, lane, LANES), axis=-1, keepdims=True)
    e = jnp.exp(m2 - m1)
    w1 = 1.0 / (1.0 + e)
    idx_ref[...] = jnp.where(lane == 0, i1, jnp.where(lane == 1, i2, 0))
    wts_ref[...] = jnp.where(lane == 0, w1, jnp.where(lane == 1, e * w1, 0.0))


def _router(x, gain, w_router, *, tm=512):
    m, d = x.shape
    tm = min(tm, m)
    assert m % tm == 0
    w = jnp.pad(w_router, ((0, 0), (0, LANES - w_router.shape[1])))
    return pl.pallas_call(
        _router_kernel,
        out_shape=(jax.ShapeDtypeStruct((m, LANES), jnp.int32), jax.ShapeDtypeStruct((m, LANES), F32)),
        grid=(m // tm,),
        in_specs=[pl.BlockSpec((tm, d), lambda i: (i, 0)), pl.BlockSpec((1, d), lambda i: (0, 0)),
                  pl.BlockSpec((d, LANES), lambda i: (0, 0))],
        out_specs=(pl.BlockSpec((tm, LANES), lambda i: (i, 0)), pl.BlockSpec((tm, LANES), lambda i: (i, 0))),
        compiler_params=_params("parallel"),
        name="router",
    )(x, gain.reshape(1, d), w)


def _row_copy(src, dst, sem, i, j):
    return pltpu.make_async_copy(src.at[pl.ds(i, 1)], dst.at[pl.ds(j, 1)], sem)


def _moe_ffn_kernel(te_ref, na_ref, src_ref, x_ref, g_ref, wg_ref, wu_ref, wd_ref, o_ref,
                    xbuf_ref, xn_ref, acc_ref, sem, *, tm, nf, hp):
    t, f = pl.program_id(0), pl.program_id(1)
    last = f == pl.num_programs(1) - 1
    n_active = na_ref[0]
    active = t < n_active
    slot = t % 2

    def start_gather(tile, dst_slot):
        def issue(r, carry):
            _row_copy(x_ref, xbuf_ref.at[dst_slot], sem.at[dst_slot], src_ref[tile * tm + r], r).start()
            return carry

        lax.fori_loop(0, tm, issue, 0, unroll=_ISSUE_UNROLL)

    @pl.when(active)
    def _():
        @pl.when(f == 0)
        def _():
            @pl.when(t == 0)
            def _():
                start_gather(0, 0)

            pltpu.make_async_copy(x_ref.at[pl.ds(0, tm)], xbuf_ref.at[slot], sem.at[slot]).wait()
            xn_ref[...] = _rms(xbuf_ref[slot], g_ref[...]).astype(xn_ref.dtype)
            acc_ref[...] = jnp.zeros_like(acc_ref)

        nxt = jnp.minimum(t + 1, n_active - 1)
        share = tm // nf
        for r in range(share):
            row = f * share + r
            _row_copy(x_ref, xbuf_ref.at[1 - slot], sem.at[1 - slot], src_ref[nxt * tm + row], row).start()

        acc_ref[...] += _swiglu_step(xn_ref[...], wg_ref[0], wu_ref[0], wd_ref[0], hp)

        @pl.when(last)
        def _():
            o_ref[...] = acc_ref[...]

            @pl.when(t == n_active - 1)
            def _():
                pltpu.make_async_copy(x_ref.at[pl.ds(0, tm)], xbuf_ref.at[1 - slot], sem.at[1 - slot]).wait()

    @pl.when(jnp.logical_and(jnp.logical_not(active), last))
    def _():
        o_ref[...] = jnp.zeros_like(o_ref)


def _moe_ffn(x, src_token, gain, wg, wu, wd, tile_expert, n_active, *, tm, tf, hp=False):
    d = x.shape[1]
    s = src_token.shape[0]
    dff = wg.shape[2]
    assert s % tm == 0 and dff % tf == 0
    wdt = F32 if hp else BF16
    return pl.pallas_call(
        functools.partial(_moe_ffn_kernel, tm=tm, nf=dff // tf, hp=hp),
        out_shape=jax.ShapeDtypeStruct((s, d), F32),
        grid_spec=pltpu.PrefetchScalarGridSpec(
            num_scalar_prefetch=3,
            grid=(s // tm, dff // tf),
            in_specs=[
                pl.BlockSpec(memory_space=pl.ANY),
                pl.BlockSpec((1, d), lambda t, f, te, na, src: (0, 0)),
                pl.BlockSpec((1, d, tf), lambda t, f, te, na, src: (te[t], 0, f)),
                pl.BlockSpec((1, d, tf), lambda t, f, te, na, src: (te[t], 0, f)),
                pl.BlockSpec((1, tf, d), lambda t, f, te, na, src: (te[t], f, 0)),
            ],
            out_specs=pl.BlockSpec((tm, d), lambda t, f, te, na, src: (t, 0)),
            scratch_shapes=[pltpu.VMEM((2, tm, d), F32), pltpu.VMEM((tm, d), wdt), pltpu.VMEM((tm, d), F32),
                            pltpu.SemaphoreType.DMA((2,))],
        ),
        compiler_params=_params("arbitrary", "arbitrary"),
        name="moe_ffn",
    )(tile_expert, n_active, src_token, x, gain.reshape(1, d), wg.astype(wdt), wu.astype(wdt), wd.astype(wdt))


def _combine_kernel(slot_ref, x_ref, wts_ref, *refs, tc, has_norm):
    if has_norm:
        gain_ref, y_ref, o_ref, buf_ref, sem = refs
    else:
        y_ref, o_ref, buf_ref, sem = refs
    i = pl.program_id(0)
    slot = i % 2

    def start_gather(step, dst_slot):
        def issue(r, carry):
            for k in range(2):
                _row_copy(y_ref, buf_ref.at[dst_slot, k], sem.at[dst_slot], slot_ref[2 * (step * tc + r) + k], r).start()
            return carry

        lax.fori_loop(0, tc, issue, 0, unroll=_ISSUE_UNROLL)

    @pl.when(i == 0)
    def _():
        start_gather(0, 0)

    @pl.when(i + 1 < pl.num_programs(0))
    def _():
        start_gather(i + 1, 1 - slot)

    for k in range(2):
        pltpu.make_async_copy(y_ref.at[pl.ds(0, tc)], buf_ref.at[slot, k], sem.at[slot]).wait()
    w = wts_ref[...]
    out = x_ref[...] + (w[:, 0:1] * buf_ref[slot, 0] + w[:, 1:2] * buf_ref[slot, 1])
    if has_norm:
        out = _rms(out, gain_ref[...])
    o_ref[...] = out


def _combine(x, wts, y_sorted, slots, final_gain, *, tc=256):
    m, d = x.shape
    tc = min(tc, m)
    assert m % tc == 0
    has_norm = final_gain is not None
    args = [slots, x, wts]
    in_specs = [pl.BlockSpec((tc, d), lambda i, sl: (i, 0)), pl.BlockSpec((tc, LANES), lambda i, sl: (i, 0))]
    if has_norm:
        args.append(final_gain.reshape(1, d))
        in_specs.append(pl.BlockSpec((1, d), lambda i, sl: (0, 0)))
    args.append(y_sorted)
    in_specs.append(pl.BlockSpec(memory_space=pl.ANY))
    return pl.pallas_call(
        functools.partial(_combine_kernel, tc=tc, has_norm=has_norm),
        out_shape=jax.ShapeDtypeStruct((m, d), F32),
        grid_spec=pltpu.PrefetchScalarGridSpec(
            num_scalar_prefetch=1,
            grid=(m // tc,),
            in_specs=in_specs,
            out_specs=pl.BlockSpec((tc, d), lambda i, sl: (i, 0)),
            scratch_shapes=[pltpu.VMEM((2, 2, tc, d), F32), pltpu.SemaphoreType.DMA((2,))],
        ),
        compiler_params=_params("arbitrary"),
        name="moe_combine",
    )(*args)


def _moe_layer(x, gain, w_router, wg, wu, wd, final_gain, *, tm, tf, hp=False):
    m, d = x.shape
    n_e = wg.shape[0]
    idx, wts = _router(x, gain, w_router)
    e_flat = idx[:, :2].reshape(-1)
    onehot = (e_flat[:, None] == jnp.arange(n_e, dtype=jnp.int32)[None, :]).astype(jnp.int32)
    csum = jnp.cumsum(onehot, axis=0)
    rank = jnp.sum(csum * onehot, axis=1) - 1
    tiles_e = (csum[-1] + tm - 1) // tm
    tile_end = jnp.cumsum(tiles_e)
    slot = ((tile_end - tiles_e)[e_flat] * tm + rank).astype(jnp.int32)
    n_tiles = (2 * m) // tm + n_e
    tile_expert = jnp.minimum(
        jnp.sum(jnp.arange(n_tiles, dtype=jnp.int32)[:, None] >= tile_end[None, :], axis=1), n_e - 1).astype(jnp.int32)
    n_active = tile_end[-1:].astype(jnp.int32)
    src_token = jnp.zeros((n_tiles * tm,), jnp.int32).at[slot].set(jnp.arange(2 * m, dtype=jnp.int32) // 2)
    ys = _moe_ffn(x, src_token, gain, wg, wu, wd, tile_expert, n_active, tm=tm, tf=tf, hp=hp)
    return _combine(x, wts, ys, slot, final_gain)


def _gdn_weights(w_in, a_log, dt_bias):
    h_n = GDN_HEADS
    main = 2 * h_n * GDN_DK + 2 * h_n * GDN_DV
    w_gate = jnp.pad(w_in[:, main:], ((0, 0), (0, LANES - 2 * h_n)))
    pad = (h_n, LANES - 2 * h_n)
    alog_row = jnp.pad(a_log.astype(F32), pad).reshape(1, LANES)
    dtb_row = jnp.pad(dt_bias.astype(F32), pad).reshape(1, LANES)
    return w_in[:, :main], w_gate, alog_row, dtb_row


def _gdn_layer_prompt(x, gain, w_in, conv_w, a_log, dt_bias, o_gain, w_out, *, batch, seq):
    conv_dim = GDN_HEADS * (2 * GDN_DK + GDN_DV)
    w_main, w_gate, alog_row, dtb_row = _gdn_weights(w_in, a_log, dt_bias)
    proj = _matmul(x, w_main, gain=gain, tm=512, name="gdn_in_proj")
    gates = _matmul(x, w_gate, gain=gain, hp=True, tm=512, name="gdn_gate_proj")
    o, s_fin = _gdn_prompt(proj, gates, conv_w, alog_row, dtb_row, o_gain, batch=batch, seq=seq)
    y = _matmul(o, w_out, res=x, tm=512, name="gdn_out_proj")
    conv_state = proj.reshape(batch, seq, -1)[:, seq - (CONV_W - 1):, :conv_dim]
    return y, conv_state, s_fin


def _gdn_layer_step(x, gain, conv_prev, s_prev, w_in, conv_w, a_log, dt_bias, o_gain, w_out):
    w_main, w_gate, alog_row, dtb_row = _gdn_weights(w_in, a_log, dt_bias)
    proj = _matmul(x, w_main, gain=gain, hp=True, tn=1024, name="gdn_in_proj_s")
    gates = _matmul(x, w_gate, gain=gain, hp=True, name="gdn_gate_proj_s")
    o, conv_new, s_new = _gdn_step(proj, gates, conv_prev, s_prev, conv_w, alog_row, dtb_row, o_gain)
    y = _matmul(o.reshape(x.shape[0], -1), w_out, res=x, hp=True, name="gdn_out_proj_s")
    return y, conv_new, s_new


def _moba_layer_prompt(x, gain, w_in, w_out, *, batch, seq):
    proj = _matmul(x, w_in, gain=gain, tm=512, name="att_in_proj")
    cos_t, sin_a, sin_b = _rope_tables(jnp.arange(seq, dtype=jnp.int32))
    o, k_rot, v = _moba_prompt(proj, cos_t, sin_a, sin_b, batch=batch, seq=seq)
    y = _matmul(o, w_out, res=x, tm=512, name="att_out_proj")
    return y, k_rot, v


def _moba_layer_step(x, gain, cache_k, cache_v, page_table, w_in, w_out):
    b = x.shape[0]
    hd = HEAD_DIM
    n_pool, page = cache_k.shape[0], cache_k.shape[1]
    past_len = page_table.shape[1] * page
    own = past_len // MOBA_BLOCK
    assert own * MOBA_BLOCK == past_len and own >= 1
    n_sel = min(MOBA_TOPK, own)
    ppb = MOBA_BLOCK // page
    proj = _matmul(x, w_in, gain=gain, hp=True, tn=1024, name="att_in_proj_s")
    ck = cache_k.reshape(n_pool, page * KV_HEADS, hd)
    cv = cache_v.reshape(n_pool, page * KV_HEADS, hd)
    kmeans = _cache_block_means(ck, page_table)
    cos_t, sin_a, sin_b = _rope_tables(jnp.full((1,), past_len, jnp.int32))
    q, k_rot, sel = _moba_select(proj, kmeans, cos_t, sin_a, sin_b, n_sel=n_sel)
    logical = sel[:, :, :n_sel, None] * ppb + jnp.arange(ppb, dtype=jnp.int32)
    pages = jnp.take_along_axis(page_table, logical.reshape(b, -1), axis=1).reshape(-1)
    v_new = proj[:, (ATT_HEADS + KV_HEADS) * hd:]
    o = _moba_step(q, k_rot.reshape(b, KV_HEADS, hd), v_new.reshape(b, KV_HEADS, hd), ck, cv, pages)
    y = _matmul(o, w_out, res=x, hp=True, name="att_out_proj_s")
    return y, k_rot.reshape(b, KV_HEADS * hd), v_new


def _norm_kernel(x_ref, g_ref, o_ref):
    o_ref[...] = _rms(x_ref[...], g_ref[...])


def _final_norm(x, gain, *, tm=512):
    m, d = x.shape
    tm = min(tm, m)
    return pl.pallas_call(
        _norm_kernel,
        out_shape=jax.ShapeDtypeStruct((m, d), F32),
        grid=(m // tm,),
        in_specs=[pl.BlockSpec((tm, d), lambda i: (i, 0)), pl.BlockSpec((1, d), lambda i: (0, 0))],
        out_specs=pl.BlockSpec((tm, d), lambda i: (i, 0)),
        compiler_params=_params("parallel"),
        name="final_norm",
    )(x, gain.reshape(1, d))


def kernel(x_prompt, x_sample, state_conv, state_delta, cache_k, cache_v, page_table, norm_mix, norm_ffn,
           norm_final, gdn_w_in, gdn_conv_w, gdn_a_log, gdn_dt_bias, gdn_o_norm, gdn_w_out, att_w_in,
           att_w_out, ffn_w_gate, ffn_w_up, ffn_w_down, moe_router, moe_w_gate, moe_w_up, moe_w_down):
    bp, seq, d = x_prompt.shape
    db, dec_seq, _ = x_sample.shape
    assert dec_seq == 1
    depth = norm_mix.shape[0]
    hp = x_prompt.reshape(bp * seq, d)
    hs = x_sample.reshape(db, d)
    conv_p, delta_p, k_p, v_p = [], [], [], []
    conv_s, delta_s, k_s, v_s = [], [], [], []
    for i in range(depth):
        j = i // 2
        last = i == depth - 1
        if i % 2 == 0:
            gdn = (gdn_w_in[j], gdn_conv_w[j], gdn_a_log[j], gdn_dt_bias[j], gdn_o_norm[j], gdn_w_out[j])
            hp, cp, sp = _gdn_layer_prompt(hp, norm_mix[i], *gdn, batch=bp, seq=seq)
            hs, cs, ss = _gdn_layer_step(hs, norm_mix[i], state_conv[j], state_delta[j], *gdn)
            conv_p.append(cp)
            delta_p.append(sp.astype(state_delta.dtype))
            conv_s.append(cs)
            delta_s.append(ss.astype(state_delta.dtype))
            ffn = (norm_ffn[i], ffn_w_gate[j], ffn_w_up[j], ffn_w_down[j])
            hp = _ffn(hp, *ffn)
            hs = _ffn(hs, *ffn, hp=True)
            if last:
                hp = _final_norm(hp, norm_final)
                hs = _final_norm(hs, norm_final)
        else:
            hp, kp, vp = _moba_layer_prompt(hp, norm_mix[i], att_w_in[j], att_w_out[j], batch=bp, seq=seq)
            hs, kn, vn = _moba_layer_step(hs, norm_mix[i], cache_k[j], cache_v[j], page_table,
                                          att_w_in[j], att_w_out[j])
            k_p.append(kp.reshape(bp, seq, KV_HEADS, HEAD_DIM))
            v_p.append(vp.reshape(bp, seq, KV_HEADS, HEAD_DIM))
            k_s.append(kn.reshape(db, 1, KV_HEADS, HEAD_DIM))
            v_s.append(vn.reshape(db, 1, KV_HEADS, HEAD_DIM))
            moe = (norm_ffn[i], moe_router[j], moe_w_gate[j], moe_w_up[j], moe_w_down[j])
            fg = norm_final if last else None
            hp = _moe_layer(hp, *moe, fg, tm=512, tf=1792)
            hs = _moe_layer(hs, *moe, fg, tm=32, tf=1792 if last else 512, hp=not last)
    return (hp.reshape(bp, seq, d), hs.reshape(db, 1, d),
            jnp.stack(conv_p), jnp.stack(delta_p), jnp.stack(k_p), jnp.stack(v_p),
            jnp.stack(conv_s), jnp.stack(delta_s), jnp.stack(k_s), jnp.stack(v_s))
```

```python
import functools
import math

import jax
import jax.numpy as jnp
from jax import lax
from jax.experimental import pallas as pl
from jax.experimental.pallas import tpu as pltpu

F32 = jnp.float32
BF16 = jnp.bfloat16

GDN_HEADS = 8
GDN_DK = 128
GDN_DV = 128
CONV_W = 4
GDN_CHUNK = 64
ATT_HEADS = 8
KV_HEADS = 4
HEAD_DIM = 128
ROT_DIM = HEAD_DIM // 4
ROPE_THETA = 500000.0
MOBA_BLOCK = 256
MOBA_TOPK = 3
N_EXPERTS = 8
EPS = 1e-6

LANES = 128
SUBLANES = 8
VMEM_LIMIT = 56 * 1024 * 1024

NEG_INF = float("-inf")
_ISSUE_UNROLL = 8


def _params(*sem):
    return pltpu.CompilerParams(dimension_semantics=sem, vmem_limit_bytes=VMEM_LIMIT)


def _rms(x, gain):
    return x * lax.rsqrt(jnp.mean(x * x, axis=-1, keepdims=True) + EPS) * gain


def _silu(x):
    return x * jax.nn.sigmoid(x)


def _softplus(x):
    return jnp.maximum(x, 0.0) + jnp.log1p(jnp.exp(-jnp.abs(x)))


def _split2(a):
    hi = a.astype(BF16)
    lo = (a - hi.astype(F32)).astype(BF16)
    return hi, lo


def _split3(a):
    hi = a.astype(BF16)
    r = a - hi.astype(F32)
    mid = r.astype(BF16)
    lo = (r - mid.astype(F32)).astype(BF16)
    return hi, mid, lo


_NN = (((1,), (0,)), ((), ()))
_NT = (((1,), (1,)), ((), ()))
_TN = (((0,), (0,)), ((), ()))


def _dot(a, b, dims=_NN):
    return lax.dot_general(a, b, dims, preferred_element_type=F32)


def _dot_hp(a, b, dims=_NN):
    ah, al = _split2(a)
    bh, bl = _split2(b)
    return _dot(ah, bh, dims) + (_dot(al, bh, dims) + _dot(ah, bl, dims))


def _mm_kernel(*refs, has_gain, has_res, hp):
    it = iter(refs)
    a_ref = next(it)
    g_ref = next(it) if has_gain else None
    w_ref = next(it)
    r_ref = next(it) if has_res else None
    o_ref = next(it)
    an_ref = next(it)

    @pl.when(pl.program_id(1) == 0)
    def _():
        a = a_ref[...].astype(F32)
        if has_gain:
            a = _rms(a, g_ref[...])
        an_ref[...] = a.astype(an_ref.dtype)

    acc = (_dot_hp if hp else _dot)(an_ref[...], w_ref[...])
    if has_res:
        acc = acc + r_ref[...]
    o_ref[...] = acc.astype(o_ref.dtype)


def _matmul(a, w, *, gain=None, res=None, hp=False, tm=512, tn=None, out_dtype=F32, name="matmul"):
    m, k = a.shape
    n = w.shape[1]
    tm = min(tm, m)
    tn = n if tn is None else min(tn, n)
    assert m % tm == 0 and n % tn == 0
    args = [a]
    in_specs = [pl.BlockSpec((tm, k), lambda i, j: (i, 0))]
    if gain is not None:
        args.append(gain.reshape(1, k).astype(F32))
        in_specs.append(pl.BlockSpec((1, k), lambda i, j: (0, 0)))
    args.append(w if hp else w.astype(BF16))
    in_specs.append(pl.BlockSpec((k, tn), lambda i, j: (0, j)))
    if res is not None:
        args.append(res)
        in_specs.append(pl.BlockSpec((tm, tn), lambda i, j: (i, j)))
    scratch = [pltpu.VMEM((tm, k), F32 if hp else BF16)]
    return pl.pallas_call(
        functools.partial(_mm_kernel, has_gain=gain is not None, has_res=res is not None, hp=hp),
        out_shape=jax.ShapeDtypeStruct((m, n), out_dtype),
        grid=(m // tm, n // tn),
        in_specs=in_specs,
        out_specs=pl.BlockSpec((tm, tn), lambda i, j: (i, j)),
        scratch_shapes=scratch,
        compiler_params=_params("parallel", "arbitrary"),
        name=name,
    )(*args)


def _swiglu_step(xn, wg, wu, wd, hp):
    dot = _dot_hp if hp else _dot
    act = _silu(dot(xn, wg)) * dot(xn, wu)
    return dot(act if hp else act.astype(BF16), wd)


def _ffn_kernel(x_ref, g_ref, wg_ref, wu_ref, wd_ref, o_ref, xn_ref, acc_ref, *, hp):
    f = pl.program_id(1)

    @pl.when(f == 0)
    def _():
        xn_ref[...] = _rms(x_ref[...], g_ref[...]).astype(xn_ref.dtype)
        acc_ref[...] = jnp.zeros_like(acc_ref)

    acc_ref[...] += _swiglu_step(xn_ref[...], wg_ref[...], wu_ref[...], wd_ref[...], hp)

    @pl.when(f == pl.num_programs(1) - 1)
    def _():
        o_ref[...] = x_ref[...] + acc_ref[...]


def _ffn(x, gain, wg, wu, wd, *, hp=False, tm=1024, tf=512):
    m, d = x.shape
    dff = wg.shape[1]
    tm = min(tm, m)
    assert m % tm == 0 and dff % tf == 0
    wdt = F32 if hp else BF16
    return pl.pallas_call(
        functools.partial(_ffn_kernel, hp=hp),
        out_shape=jax.ShapeDtypeStruct((m, d), F32),
        grid=(m // tm, dff // tf),
        in_specs=[
            pl.BlockSpec((tm, d), lambda i, f: (i, 0)),
            pl.BlockSpec((1, d), lambda i, f: (0, 0)),
            pl.BlockSpec((d, tf), lambda i, f: (0, f)),
            pl.BlockSpec((d, tf), lambda i, f: (0, f)),
            pl.BlockSpec((tf, d), lambda i, f: (f, 0)),
        ],
        out_specs=pl.BlockSpec((tm, d), lambda i, f: (i, 0)),
        scratch_shapes=[pltpu.VMEM((tm, d), wdt), pltpu.VMEM((tm, d), F32)],
        compiler_params=_params("parallel", "arbitrary"),
        name="ffn",
    )(x, gain.reshape(1, d), wg.astype(wdt), wu.astype(wdt), wd.astype(wdt))


def _bdot(a, b, dims=_NN):
    return _dot(a.astype(BF16), b.astype(BF16), dims)


def _neumann_inverse_minus_eye(mats):
    c = mats[0].shape[0]
    ps = [-a for a in mats]
    ns = list(ps)
    for _ in range(int(math.log2(c)) - 1):
        ps = [_bdot(p, p) for p in ps]
        ns = [n + p + _bdot(n, p) for n, p in zip(ns, ps)]
    return ns


def _gdn_kernel(proj_ref, gate_ref, convw_ref, alog_ref, dtb_ref, ogain_ref,
                o_ref, sfin_ref, s_ref, tail_ref, *, tt, c):
    h_n, dk, dv = GDN_HEADS, GDN_DK, GDN_DV
    t = pl.program_id(1)

    @pl.when(t == 0)
    def _():
        s_ref[...] = jnp.zeros_like(s_ref)
        tail_ref[...] = jnp.zeros_like(tail_ref)

    gt = gate_ref[...]
    beta_all = jax.nn.sigmoid(gt)
    g_all = -jnp.exp(alog_ref[...]) * _softplus(gt + dtb_ref[...])

    ri = lax.broadcasted_iota(jnp.int32, (c, c), 0)
    ci = lax.broadcasted_iota(jnp.int32, (c, c), 1)
    tri = ri >= ci
    strict = ri > ci
    tri_b = tri.astype(BF16)
    ri2 = lax.broadcasted_iota(jnp.int32, (c, LANES), 0)
    ci2 = lax.broadcasted_iota(jnp.int32, (c, LANES), 1)
    upper_b = (ri2 <= ci2).astype(BF16)

    n_chunks = tt // c
    gams, gam_ts = [], []
    for ck in range(n_chunks):
        parts = _split3(g_all[ck * c:(ck + 1) * c, :])
        gam = sum(_dot(tri_b, p) for p in parts)
        gam_t = sum(_dot(p, upper_b, _TN) for p in parts)
        gams.append(gam)
        gam_ts.append(gam_t)

    def conv_silu(off):
        raw = proj_ref[:, off:off + LANES]
        xs = jnp.concatenate([tail_ref[:, off:off + LANES], raw], axis=0)
        w = convw_ref[:, off:off + LANES]
        acc = raw * w[CONV_W - 1:CONV_W]
        for j in range(CONV_W - 1):
            lo = SUBLANES - (CONV_W - 1) + j
            acc = acc + xs[lo:lo + tt] * w[j:j + 1]
        return _silu(acc)

    def l2n(x):
        return x * lax.rsqrt(jnp.sum(x * x, axis=-1, keepdims=True) + 1e-6)

    heads = range(h_n)
    pairs = [(ck, h) for ck in range(n_chunks) for h in heads]
    z_off = 2 * h_n * dk + h_n * dv
    q_all = [l2n(conv_silu(h * dk)) * (dk ** -0.5) for h in heads]
    k_all = [l2n(conv_silu(h_n * dk + h * dk)) for h in heads]
    v_all = [conv_silu(2 * h_n * dk + h * dv) for h in heads]

    def rows(x, ck):
        return x[ck * c:(ck + 1) * c]

    qc = [rows(q_all[h], ck) for ck, h in pairs]
    kc = [rows(k_all[h], ck) for ck, h in pairs]
    bcol = [rows(beta_all, ck)[:, h:h + 1] for ck, h in pairs]
    col = [gams[ck][:, h_n + h:h_n + h + 1] for ck, h in pairs]
    dm = [jnp.exp(jnp.where(tri, cl - gam_ts[ck][h_n + h:h_n + h + 1, :c], NEG_INF))
          for cl, (ck, h) in zip(col, pairs)]
    kb = [x * b for x, b in zip(kc, bcol)]
    kq = [_bdot(jnp.concatenate([x, y], axis=0), z, _NT) for x, y, z in zip(kb, qc, kc)]
    a = [jnp.where(strict, x[:c] * d, 0.0) for x, d in zip(kq, dm)]
    attn = [(x[c:] * d).astype(BF16) for x, d in zip(kq, dm)]
    ninv = _neumann_inverse_minus_eye(a)
    eg = [jnp.exp(cl) for cl in col]
    rhs = [jnp.concatenate([rows(v_all[h], ck) * b, x * e], axis=1)
           for (ck, h), b, x, e in zip(pairs, bcol, kb, eg)]
    sol = [r + _bdot(n, r) for n, r in zip(ninv, rhs)]
    glast = [cl[c - 1:c, :] for cl in col]
    lhs = [jnp.concatenate([x[:, dv:], y * e], axis=0).astype(BF16)
           for x, y, e in zip(sol, qc, eg)]
    k_dec = [(x * jnp.exp(g - cl)).astype(BF16) for x, g, cl in zip(kc, glast, col)]
    g_end = [jnp.exp(g) for g in glast]

    ogain = ogain_ref[...]
    for ck in range(n_chunks):
        idx = [ck * h_n + h for h in heads]
        s_old = [s_ref[h] for h in heads]
        both = [_dot(lhs[i], s.astype(BF16)) for i, s in zip(idx, s_old)]
        ub = [(sol[i][:, :dv] - b[:c]).astype(BF16) for i, b in zip(idx, both)]
        o = [b[c:] + _dot(attn[i], u) for i, b, u in zip(idx, both, ub)]
        for h, i, s, u in zip(heads, idx, s_old, ub):
            s_ref[h] = s * g_end[i] + _dot(k_dec[i], u, _TN)
        for h, x in zip(heads, o):
            zc = proj_ref[ck * c:(ck + 1) * c, z_off + h * dv:z_off + (h + 1) * dv]
            o_ref[ck * c:(ck + 1) * c, h * dv:(h + 1) * dv] = (_rms(x, ogain) * _silu(zc)).astype(o_ref.dtype)

    tail_ref[...] = proj_ref[tt - SUBLANES:tt, :tail_ref.shape[1]]

    @pl.when(t == pl.num_programs(1) - 1)
    def _():
        sfin_ref[0] = s_ref[...]


def _gdn_prompt(proj, gates, conv_w, alog_row, dtb_row, o_gain, *, batch, seq, tt=256):
    h_n, dk, dv = GDN_HEADS, GDN_DK, GDN_DV
    conv_dim = h_n * (2 * dk + dv)
    width = proj.shape[1]
    tt = min(tt, seq)
    c = math.gcd(seq, GDN_CHUNK)
    assert seq % tt == 0 and tt % c == 0 and tt >= SUBLANES
    nt = seq // tt
    return pl.pallas_call(
        functools.partial(_gdn_kernel, tt=tt, c=c),
        out_shape=(jax.ShapeDtypeStruct((batch * seq, h_n * dv), BF16),
                   jax.ShapeDtypeStruct((batch, h_n, dk, dv), F32)),
        grid=(batch, nt),
        in_specs=[
            pl.BlockSpec((tt, width), lambda b, t: (b * nt + t, 0)),
            pl.BlockSpec((tt, LANES), lambda b, t: (b * nt + t, 0)),
            pl.BlockSpec((CONV_W, conv_dim), lambda b, t: (0, 0)),
            pl.BlockSpec((1, LANES), lambda b, t: (0, 0)),
            pl.BlockSpec((1, LANES), lambda b, t: (0, 0)),
            pl.BlockSpec((1, dv), lambda b, t: (0, 0)),
        ],
        out_specs=(pl.BlockSpec((tt, h_n * dv), lambda b, t: (b * nt + t, 0)),
                   pl.BlockSpec((1, h_n, dk, dv), lambda b, t: (b, 0, 0, 0))),
        scratch_shapes=[pltpu.VMEM((h_n, dk, dv), F32), pltpu.VMEM((SUBLANES, conv_dim), F32)],
        compiler_params=_params("parallel", "arbitrary"),
        name="gdn_prompt",
    )(proj, gates, conv_w, alog_row, dtb_row, o_gain.reshape(1, dv))


def _gdn_step_kernel(proj_ref, gate_ref, cprev_ref, s_ref, convw_ref, alog_ref, dtb_ref, ogain_ref,
                     o_ref, cnew_ref, snew_ref):
    h_n, dk, dv = GDN_HEADS, GDN_DK, GDN_DV
    conv_dim = h_n * (2 * dk + dv)
    u_row = proj_ref[0]
    prev = cprev_ref[0]
    w = convw_ref[...]
    raw = u_row[:, :conv_dim]
    acc = raw * w[CONV_W - 1:CONV_W]
    for j in range(CONV_W - 1):
        acc = acc + prev[j:j + 1] * w[j:j + 1]
    qkv = _silu(acc)
    for j in range(CONV_W - 2):
        cnew_ref[0, j:j + 1, :] = prev[j + 1:j + 2]
    cnew_ref[0, CONV_W - 2:CONV_W - 1, :] = raw

    gt = gate_ref[0]
    beta_all = jax.nn.sigmoid(gt)
    g_all = -jnp.exp(alog_ref[...]) * _softplus(gt + dtb_ref[...])
    rows = lax.broadcasted_iota(jnp.int32, (SUBLANES, LANES), 0)
    ogain = ogain_ref[...]

    def l2n(x):
        return x * lax.rsqrt(jnp.sum(x * x, axis=-1, keepdims=True) + 1e-6)

    def as_row0(x, y=None):
        out = jnp.where(rows == 0, jnp.broadcast_to(x, (SUBLANES, LANES)), 0.0)
        if y is not None:
            out = jnp.where(rows == 1, jnp.broadcast_to(y, (SUBLANES, LANES)), out)
        return out

    for h in range(h_n):
        q = l2n(qkv[:, h * dk:(h + 1) * dk]) * (dk ** -0.5)
        k = l2n(qkv[:, h_n * dk + h * dk:h_n * dk + (h + 1) * dk])
        v = qkv[:, 2 * h_n * dk + h * dv:2 * h_n * dk + (h + 1) * dv]
        z = u_row[:, conv_dim + h * dv:conv_dim + (h + 1) * dv]
        beta = beta_all[:, h:h + 1]
        eg = jnp.exp(g_all[:, h_n + h:h_n + h + 1])
        s = s_ref[0, h]
        kcd = k * (beta * eg)
        q_dec = q * eg
        both = _dot_hp(as_row0(kcd, q_dec), s)
        u = v * beta - both[0:1]
        o = both[1:2] + jnp.sum(q * k, axis=-1, keepdims=True) * u
        snew_ref[0, h] = s * eg + _dot_hp(as_row0(k), as_row0(u), _TN)
        o_ref[0, :, h * dv:(h + 1) * dv] = (_rms(o, ogain) * _silu(z)).astype(o_ref.dtype)


def _gdn_step(proj, gates, conv_prev, s_prev, conv_w, alog_row, dtb_row, o_gain):
    h_n, dk, dv = GDN_HEADS, GDN_DK, GDN_DV
    conv_dim = h_n * (2 * dk + dv)
    b, width = proj.shape
    return pl.pallas_call(
        _gdn_step_kernel,
        out_shape=(jax.ShapeDtypeStruct((b, 1, h_n * dv), F32),
                   jax.ShapeDtypeStruct((b, CONV_W - 1, conv_dim), F32),
                   jax.ShapeDtypeStruct((b, h_n, dk, dv), F32)),
        grid=(b,),
        in_specs=[
            pl.BlockSpec((1, 1, width), lambda i: (i, 0, 0)),
            pl.BlockSpec((1, 1, LANES), lambda i: (i, 0, 0)),
            pl.BlockSpec((1, CONV_W - 1, conv_dim), lambda i: (i, 0, 0)),
            pl.BlockSpec((1, h_n, dk, dv), lambda i: (i, 0, 0, 0)),
            pl.BlockSpec((CONV_W, conv_dim), lambda i: (0, 0)),
            pl.BlockSpec((1, LANES), lambda i: (0, 0)),
            pl.BlockSpec((1, LANES), lambda i: (0, 0)),
            pl.BlockSpec((1, dv), lambda i: (0, 0)),
        ],
        out_specs=(pl.BlockSpec((1, 1, h_n * dv), lambda i: (i, 0, 0)),
                   pl.BlockSpec((1, CONV_W - 1, conv_dim), lambda i: (i, 0, 0)),
                   pl.BlockSpec((1, h_n, dk, dv), lambda i: (i, 0, 0, 0))),
        compiler_params=_params("parallel"),
        name="gdn_step",
    )(proj.reshape(b, 1, width), gates.reshape(b, 1, LANES), conv_prev, s_prev, conv_w,
      alog_row, dtb_row, o_gain.reshape(1, dv))


def _rope(x, cos, sin_a, sin_b):
    half = ROT_DIM // 2
    return (x * cos + pltpu.roll(x, HEAD_DIM - half, 1) * sin_a + pltpu.roll(x, half, 1) * sin_b)


def _rope_tables(pos):
    half = ROT_DIM // 2
    inv = jnp.power(ROPE_THETA, -jnp.arange(half, dtype=F32) * (2.0 / ROT_DIM))
    ang = pos.astype(F32)[:, None] * inv[None, :]
    cos, sin = jnp.cos(ang), jnp.sin(ang)
    n = pos.shape[0]
    rest = HEAD_DIM - ROT_DIM
    cos_t = jnp.concatenate([cos, cos, jnp.ones((n, rest), F32)], axis=1)
    sin_a = jnp.concatenate([-sin, jnp.zeros((n, half + rest), F32)], axis=1)
    sin_b = jnp.concatenate([jnp.zeros((n, half), F32), sin, jnp.zeros((n, rest), F32)], axis=1)
    return cos_t, sin_a, sin_b


def _moba_kernel(q_ref, k_ref, v_ref, cos_ref, sa_ref, sb_ref, o_ref, kf_ref, vf_ref,
                 kout_ref, kb_ref, vb_ref, *, seq):
    blk, hd = MOBA_BLOCK, HEAD_DIM
    group = ATT_HEADS // KV_HEADS
    nb = seq // blk
    nq = group * blk
    scale = hd ** -0.5

    kout_ref[...] = _rope(k_ref[...], cos_ref[...], sa_ref[...], sb_ref[...])
    kb_ref[...] = kout_ref[...].astype(BF16)
    vb_ref[...] = v_ref[...].astype(BF16)
    for gg in range(KV_HEADS):
        @pl.when(pl.program_id(1) == gg)
        def _(gg=gg):
            kf_ref[pl.ds(gg, seq, stride=KV_HEADS), :] = kout_ref[...]
            vf_ref[pl.ds(gg, seq, stride=KV_HEADS), :] = v_ref[...]

    kmean = jnp.concatenate(
        [jnp.sum(kout_ref[n * blk:(n + 1) * blk, :], axis=0, keepdims=True) for n in range(nb)],
        axis=0) * (1.0 / blk)

    blk_id = lax.broadcasted_iota(jnp.int32, (nb, nq), 0)
    key_r = lax.broadcasted_iota(jnp.int32, (blk, nq), 0)
    qry_c = lax.broadcasted_iota(jnp.int32, (blk, nq), 1) % blk
    causal = key_r <= qry_c

    for i in range(nb):
        rows = slice(i * blk, (i + 1) * blk)
        cos, sa, sb = cos_ref[rows, :], sa_ref[rows, :], sb_ref[rows, :]
        qs = [_rope(q_ref[rows, g * hd:(g + 1) * hd], cos, sa, sb) for g in range(group)]
        q = jnp.concatenate(qs, axis=0)
        qb = (q * (scale * math.log2(math.e))).astype(BF16)

        if i > MOBA_TOPK:
            gate = jnp.where(blk_id < i, _dot_hp(kmean, q, _NT), NEG_INF)
            rank = jnp.zeros((nb, nq), F32)
            for m in range(i):
                gm = gate[m:m + 1, :]
                rank = rank + jnp.where((gm > gate) | ((gm == gate) & (m < blk_id)), 1.0, 0.0)
            sel = jnp.where((blk_id < i) & (rank < MOBA_TOPK), 1.0, 0.0)
        else:
            sel = None

        ss = []
        for j in range(i + 1):
            s = _dot(kb_ref[j * blk:(j + 1) * blk, :], qb, _NT)
            if j == i:
                s = jnp.where(causal, s, NEG_INF)
            elif sel is not None:
                s = jnp.where(sel[j:j + 1, :] > 0.0, s, NEG_INF)
            ss.append(s)
        m_all = functools.reduce(jnp.maximum, [jnp.max(s, axis=0, keepdims=True) for s in ss])
        ps = [jnp.exp2(s - m_all) for s in ss]
        l_all = functools.reduce(jnp.add, [jnp.sum(p, axis=0, keepdims=True) for p in ps])
        acc = functools.reduce(
            jnp.add, [_dot(vb_ref[j * blk:(j + 1) * blk, :], p.astype(BF16), _TN) for j, p in enumerate(ps)])
        out = (acc / l_all).T
        for g in range(group):
            o_ref[rows, g * hd:(g + 1) * hd] = out[g * blk:(g + 1) * blk].astype(o_ref.dtype)


def _moba_prompt(proj, cos_t, sin_a, sin_b, *, batch, seq):
    hd = HEAD_DIM
    group = ATT_HEADS // KV_HEADS
    assert seq % MOBA_BLOCK == 0
    tbl = pl.BlockSpec((seq, hd), lambda b, g: (0, 0))
    kv_rows = jax.ShapeDtypeStruct((batch * seq * KV_HEADS, hd), F32)
    kv_spec = pl.BlockSpec((seq * KV_HEADS, hd), lambda b, g: (b, 0))
    return pl.pallas_call(
        functools.partial(_moba_kernel, seq=seq),
        out_shape=(jax.ShapeDtypeStruct((batch * seq, ATT_HEADS * hd), BF16), kv_rows, kv_rows),
        grid=(batch, KV_HEADS),
        in_specs=[
            pl.BlockSpec((seq, group * hd), lambda b, g: (b, g)),
            pl.BlockSpec((seq, hd), lambda b, g: (b, ATT_HEADS + g)),
            pl.BlockSpec((seq, hd), lambda b, g: (b, ATT_HEADS + KV_HEADS + g)),
            tbl, tbl, tbl,
        ],
        out_specs=(pl.BlockSpec((seq, group * hd), lambda b, g: (b, g)), kv_spec, kv_spec),
        scratch_shapes=[pltpu.VMEM((seq, hd), F32), pltpu.VMEM((seq, hd), BF16), pltpu.VMEM((seq, hd), BF16)],
        compiler_params=_params("parallel", "arbitrary"),
        name="moba_prompt",
    )(proj, proj, proj, cos_t, sin_a, sin_b)


def _kmean_kernel(pt_ref, *refs, pages, pages_per_block):
    page_refs, o_ref = refs[:pages], refs[pages]
    rows = page_refs[0].shape[1]
    tokens = pages_per_block * rows // KV_HEADS
    for n in range(pages // pages_per_block):
        part = jnp.zeros((SUBLANES, HEAD_DIM), F32)
        for p in range(pages_per_block):
            x = page_refs[n * pages_per_block + p][0]
            part = part + jnp.sum(x.reshape(rows // SUBLANES, SUBLANES, HEAD_DIM), axis=0)
        acc = part[:KV_HEADS]
        for r in range(1, SUBLANES // KV_HEADS):
            acc = acc + part[r * KV_HEADS:(r + 1) * KV_HEADS]
        o_ref[0, n] = acc * (1.0 / tokens)


def _cache_block_means(cache, page_table, *, blocks_per_step=8):
    n_pool, rows, hd = cache.shape
    page = rows // KV_HEADS
    b, n_pages = page_table.shape
    ppb = MOBA_BLOCK // page
    assert ppb * page == MOBA_BLOCK and n_pages % ppb == 0 and rows % SUBLANES == 0
    n_blocks = n_pages // ppb
    bps = math.gcd(blocks_per_step, n_blocks)
    pages = bps * ppb
    specs = [pl.BlockSpec((1, rows, hd), lambda i, s, pt, p=p: (pt[i, s * pages + p], 0, 0))
             for p in range(pages)]
    return pl.pallas_call(
        functools.partial(_kmean_kernel, pages=pages, pages_per_block=ppb),
        out_shape=jax.ShapeDtypeStruct((b, n_blocks, KV_HEADS, hd), F32),
        grid_spec=pltpu.PrefetchScalarGridSpec(
            num_scalar_prefetch=1,
            grid=(b, n_blocks // bps),
            in_specs=specs,
            out_specs=pl.BlockSpec((1, bps, KV_HEADS, hd), lambda i, s, pt: (i, s, 0, 0)),
        ),
        compiler_params=_params("parallel", "arbitrary"),
        name="cache_block_means",
    )(page_table, *([cache] * pages))


def _moba_select_kernel(proj_ref, km_ref, cos_ref, sa_ref, sb_ref, q_ref, k_ref, sel_ref, *, n_sel):
    hd = HEAD_DIM
    group = ATT_HEADS // KV_HEADS
    row = proj_ref[0]
    cos, sa, sb = cos_ref[...], sa_ref[...], sb_ref[...]
    rows = lax.broadcasted_iota(jnp.int32, (SUBLANES, hd), 0)
    q = jnp.zeros((SUBLANES, hd), F32)
    for h in range(ATT_HEADS):
        qh = _rope(row[:, h * hd:(h + 1) * hd], cos, sa, sb)
        q = jnp.where(rows == h, jnp.broadcast_to(qh, (SUBLANES, hd)), q)
    q_ref[0] = q
    for g in range(KV_HEADS):
        off = (ATT_HEADS + g) * hd
        k_ref[0, :, g * hd:(g + 1) * hd] = _rope(row[:, off:off + hd], cos, sa, sb)

    nb = km_ref.shape[1]
    hrow = lax.broadcasted_iota(jnp.int32, (ATT_HEADS, nb), 0)
    lane = lax.broadcasted_iota(jnp.int32, (ATT_HEADS, nb), 1)
    gate = jnp.zeros((ATT_HEADS, nb), F32)
    for g in range(KV_HEADS):
        gg = _dot_hp(q, km_ref[0, :, g, :], _NT)
        gate = jnp.where(hrow // group == g, gg, gate)
    out_lane = lax.broadcasted_iota(jnp.int32, (ATT_HEADS, LANES), 1)
    sel = jnp.zeros((ATT_HEADS, LANES), jnp.int32)
    for t in range(n_sel):
        m = jnp.max(gate, axis=-1, keepdims=True)
        idx = jnp.min(jnp.where(gate == m, lane, nb), axis=-1, keepdims=True)
        sel = jnp.where(out_lane == t, idx, sel)
        gate = jnp.where(lane == idx, NEG_INF, gate)
    sel_ref[0] = sel


def _moba_select(proj, kmeans, cos_t, sin_a, sin_b, *, n_sel):
    b, width = proj.shape
    nb = kmeans.shape[1]
    hd = HEAD_DIM
    tbl = pl.BlockSpec((1, hd), lambda i: (0, 0))
    return pl.pallas_call(
        functools.partial(_moba_select_kernel, n_sel=n_sel),
        out_shape=(jax.ShapeDtypeStruct((b, ATT_HEADS, hd), F32),
                   jax.ShapeDtypeStruct((b, 1, KV_HEADS * hd), F32),
                   jax.ShapeDtypeStruct((b, ATT_HEADS, LANES), jnp.int32)),
        grid=(b,),
        in_specs=[pl.BlockSpec((1, 1, width), lambda i: (i, 0, 0)),
                  pl.BlockSpec((1, nb, KV_HEADS, hd), lambda i: (i, 0, 0, 0)),
                  tbl, tbl, tbl],
        out_specs=(pl.BlockSpec((1, ATT_HEADS, hd), lambda i: (i, 0, 0)),
                   pl.BlockSpec((1, 1, KV_HEADS * hd), lambda i: (i, 0, 0)),
                   pl.BlockSpec((1, ATT_HEADS, LANES), lambda i: (i, 0, 0))),
        compiler_params=_params("parallel"),
        name="moba_select",
    )(proj.reshape(b, 1, width), kmeans, cos_t, sin_a, sin_b)


def _moba_step_kernel(pg_ref, q_ref, kn_ref, vn_ref, *refs, n_pages, heads):
    n = heads * n_pages
    k_refs, v_refs, o_ref = refs[:n], refs[n:2 * n], refs[2 * n]
    hd = HEAD_DIM
    group = ATT_HEADS // KV_HEADS
    scale = hd ** -0.5
    page = k_refs[0].shape[1] // KV_HEADS

    for hh in range(heads):
        h = pl.program_id(1) * heads + hh
        g = h // group

        def head_rows(ref, g=g):
            out = ref[0, pl.ds(0, page, stride=KV_HEADS), :]
            for gg in range(1, KV_HEADS):
                out = jnp.where(g == gg, ref[0, pl.ds(gg, page, stride=KV_HEADS), :], out)
            return out

        q = q_ref[0, pl.ds(h, 1), :]
        q8 = jnp.broadcast_to(q, (SUBLANES, hd))
        kn = kn_ref[0, pl.ds(g, 1), :]
        vn = vn_ref[0, pl.ds(g, 1), :]
        s_self = jnp.sum(q * kn, axis=-1, keepdims=True) * scale
        scores = [_dot_hp(q8, head_rows(k_refs[hh * n_pages + p]), _NT)[0:1] * scale
                  for p in range(n_pages)]
        m = functools.reduce(jnp.maximum, [jnp.max(s, axis=-1, keepdims=True) for s in scores], s_self)
        p_self = jnp.exp(s_self - m)
        l = p_self
        acc = p_self * vn
        for p in range(n_pages):
            e = jnp.exp(scores[p] - m)
            l = l + jnp.sum(e, axis=-1, keepdims=True)
            acc = acc + _dot_hp(jnp.broadcast_to(e, (SUBLANES, page)), head_rows(v_refs[hh * n_pages + p]))[0:1]
        o_ref[0, hh:hh + 1, :] = (acc / l).astype(o_ref.dtype)


def _moba_step(q, k_new, v_new, cache_k, cache_v, pages, *, heads=4):
    b = q.shape[0]
    hd = HEAD_DIM
    rows = cache_k.shape[1]
    n_pages = pages.shape[0] // (b * ATT_HEADS)
    assert ATT_HEADS % heads == 0
    steps = ATT_HEADS // heads

    def page_spec(k):
        return pl.BlockSpec((1, rows, hd),
                            lambda i, s, pg, k=k: (pg[(i * ATT_HEADS + s * heads) * n_pages + k], 0, 0))

    specs = ([pl.BlockSpec((1, ATT_HEADS, hd), lambda i, s, pg: (i, 0, 0)),
              pl.BlockSpec((1, KV_HEADS, hd), lambda i, s, pg: (i, 0, 0)),
              pl.BlockSpec((1, KV_HEADS, hd), lambda i, s, pg: (i, 0, 0))]
             + [page_spec(k) for k in range(heads * n_pages)] * 2)
    out = pl.pallas_call(
        functools.partial(_moba_step_kernel, n_pages=n_pages, heads=heads),
        out_shape=jax.ShapeDtypeStruct((b * steps, heads, hd), F32),
        grid_spec=pltpu.PrefetchScalarGridSpec(
            num_scalar_prefetch=1,
            grid=(b, steps),
            in_specs=specs,
            out_specs=pl.BlockSpec((1, heads, hd), lambda i, s, pg: (i * steps + s, 0, 0)),
        ),
        compiler_params=_params("parallel", "arbitrary"),
        name="moba_step",
    )(pages, q, k_new, v_new, *([cache_k] * (heads * n_pages)), *([cache_v] * (heads * n_pages)))
    return out.reshape(b, ATT_HEADS * hd)


def _router_kernel(x_ref, g_ref, w_ref, idx_ref, wts_ref):
    logits = _dot_hp(_rms(x_ref[...], g_ref[...]), w_ref[...])
    lane = lax.broadcasted_iota(jnp.int32, logits.shape, 1)
    logits = jnp.where(lane < N_EXPERTS, logits, NEG_INF)
    m1 = jnp.max(logits, axis=-1, keepdims=True)
    i1 = jnp.min(jnp.where(logits == m1, lane, LANES), axis=-1, keepdims=True)
    rest = jnp.where(lane == i1, NEG_INF, logits)
    m2 = jnp.max(rest, axis=-1, keepdims=True)
    i2 = jnp.min(jnp.where(rest == m2, lane, LANES), axis=-1, keepdims=True)
    e = jnp.exp(m2 - m1)
    w1 = 1.0 / (1.0 + e)
    idx_ref[...] = jnp.where(lane == 0, i1, jnp.where(lane == 1, i2, 0))
    wts_ref[...] = jnp.where(lane == 0, w1, jnp.where(lane == 1, e * w1, 0.0))


def _router(x, gain, w_router, *, tm=512):
    m, d = x.shape
    tm = min(tm, m)
    assert m % tm == 0
    w = jnp.pad(w_router, ((0, 0), (0, LANES - w_router.shape[1])))
    return pl.pallas_call(
        _router_kernel,
        out_shape=(jax.ShapeDtypeStruct((m, LANES), jnp.int32), jax.ShapeDtypeStruct((m, LANES), F32)),
        grid=(m // tm,),
        in_specs=[pl.BlockSpec((tm, d), lambda i: (i, 0)), pl.BlockSpec((1, d), lambda i: (0, 0)),
                  pl.BlockSpec((d, LANES), lambda i: (0, 0))],
        out_specs=(pl.BlockSpec((tm, LANES), lambda i: (i, 0)), pl.BlockSpec((tm, LANES), lambda i: (i, 0))),
        compiler_params=_params("parallel"),
        name="router",
    )(x, gain.reshape(1, d), w)


def _row_copy(src, dst, sem, i, j):
    return pltpu.make_async_copy(src.at[pl.ds(i, 1)], dst.at[pl.ds(j, 1)], sem)


def _moe_ffn_kernel(te_ref, na_ref, src_ref, x_ref, g_ref, wg_ref, wu_ref, wd_ref, o_ref,
                    xbuf_ref, xn_ref, acc_ref, sem, *, tm, hp):
    t, f = pl.program_id(0), pl.program_id(1)
    last = f == pl.num_programs(1) - 1
    n_active = na_ref[0]
    active = t < n_active
    slot = t % 2

    def start_gather(tile, dst_slot):
        def issue(r, carry):
            _row_copy(x_ref, xbuf_ref.at[dst_slot], sem.at[dst_slot], src_ref[tile * tm + r], r).start()
            return carry

        lax.fori_loop(0, tm, issue, 0, unroll=_ISSUE_UNROLL)

    @pl.when(active)
    def _():
        @pl.when(f == 0)
        def _():
            @pl.when(t == 0)
            def _():
                start_gather(0, 0)

            pltpu.make_async_copy(x_ref.at[pl.ds(0, tm)], xbuf_ref.at[slot], sem.at[slot]).wait()
            xn_ref[...] = _rms(xbuf_ref[slot], g_ref[...]).astype(xn_ref.dtype)
            acc_ref[...] = jnp.zeros_like(acc_ref)

            @pl.when(t + 1 < n_active)
            def _():
                start_gather(t + 1, 1 - slot)

        acc_ref[...] += _swiglu_step(xn_ref[...], wg_ref[0], wu_ref[0], wd_ref[0], hp)

        @pl.when(last)
        def _():
            o_ref[...] = acc_ref[...]

    @pl.when(jnp.logical_and(jnp.logical_not(active), last))
    def _():
        o_ref[...] = jnp.zeros_like(o_ref)


def _moe_ffn(x, src_token, gain, wg, wu, wd, tile_expert, n_active, *, tm, tf, hp=False):
    d = x.shape[1]
    s = src_token.shape[0]
    dff = wg.shape[2]
    assert s % tm == 0 and dff % tf == 0
    wdt = F32 if hp else BF16
    return pl.pallas_call(
        functools.partial(_moe_ffn_kernel, tm=tm, hp=hp),
        out_shape=jax.ShapeDtypeStruct((s, d), F32),
        grid_spec=pltpu.PrefetchScalarGridSpec(
            num_scalar_prefetch=3,
            grid=(s // tm, dff // tf),
            in_specs=[
                pl.BlockSpec(memory_space=pl.ANY),
                pl.BlockSpec((1, d), lambda t, f, te, na, src: (0, 0)),
                pl.BlockSpec((1, d, tf), lambda t, f, te, na, src: (te[t], 0, f)),
                pl.BlockSpec((1, d, tf), lambda t, f, te, na, src: (te[t], 0, f)),
                pl.BlockSpec((1, tf, d), lambda t, f, te, na, src: (te[t], f, 0)),
            ],
            out_specs=pl.BlockSpec((tm, d), lambda t, f, te, na, src: (t, 0)),
            scratch_shapes=[pltpu.VMEM((2, tm, d), F32), pltpu.VMEM((tm, d), wdt), pltpu.VMEM((tm, d), F32),
                            pltpu.SemaphoreType.DMA((2,))],
        ),
        compiler_params=_params("arbitrary", "arbitrary"),
        name="moe_ffn",
    )(tile_expert, n_active, src_token, x, gain.reshape(1, d), wg.astype(wdt), wu.astype(wdt), wd.astype(wdt))


def _combine_kernel(slot_ref, x_ref, wts_ref, *refs, tc, has_norm):
    if has_norm:
        gain_ref, y_ref, o_ref, buf_ref, sem = refs
    else:
        y_ref, o_ref, buf_ref, sem = refs
    i = pl.program_id(0)
    slot = i % 2

    def start_gather(step, dst_slot):
        def issue(r, carry):
            for k in range(2):
                _row_copy(y_ref, buf_ref.at[dst_slot, k], sem.at[dst_slot], slot_ref[2 * (step * tc + r) + k], r).start()
            return carry

        lax.fori_loop(0, tc, issue, 0, unroll=_ISSUE_UNROLL)

    @pl.when(i == 0)
    def _():
        start_gather(0, 0)

    @pl.when(i + 1 < pl.num_programs(0))
    def _():
        start_gather(i + 1, 1 - slot)

    for k in range(2):
        pltpu.make_async_copy(y_ref.at[pl.ds(0, tc)], buf_ref.at[slot, k], sem.at[slot]).wait()
    w = wts_ref[...]
    out = x_ref[...] + (w[:, 0:1] * buf_ref[slot, 0] + w[:, 1:2] * buf_ref[slot, 1])
    if has_norm:
        out = _rms(out, gain_ref[...])
    o_ref[...] = out


def _combine(x, wts, y_sorted, slots, final_gain, *, tc=256):
    m, d = x.shape
    tc = min(tc, m)
    assert m % tc == 0
    has_norm = final_gain is not None
    args = [slots, x, wts]
    in_specs = [pl.BlockSpec((tc, d), lambda i, sl: (i, 0)), pl.BlockSpec((tc, LANES), lambda i, sl: (i, 0))]
    if has_norm:
        args.append(final_gain.reshape(1, d))
        in_specs.append(pl.BlockSpec((1, d), lambda i, sl: (0, 0)))
    args.append(y_sorted)
    in_specs.append(pl.BlockSpec(memory_space=pl.ANY))
    return pl.pallas_call(
        functools.partial(_combine_kernel, tc=tc, has_norm=has_norm),
        out_shape=jax.ShapeDtypeStruct((m, d), F32),
        grid_spec=pltpu.PrefetchScalarGridSpec(
            num_scalar_prefetch=1,
            grid=(m // tc,),
            in_specs=in_specs,
            out_specs=pl.BlockSpec((tc, d), lambda i, sl: (i, 0)),
            scratch_shapes=[pltpu.VMEM((2, 2, tc, d), F32), pltpu.SemaphoreType.DMA((2,))],
        ),
        compiler_params=_params("arbitrary"),
        name="moe_combine",
    )(*args)


def _invert_slots_kernel(slot_ref, src_ref, *, n_pairs, n_slots):
    def clear(s, carry):
        src_ref[s] = 0
        return carry

    def put(tok, carry):
        src_ref[slot_ref[2 * tok]] = tok
        src_ref[slot_ref[2 * tok + 1]] = tok
        return carry

    lax.fori_loop(0, n_slots, clear, 0, unroll=_ISSUE_UNROLL)
    lax.fori_loop(0, n_pairs // 2, put, 0, unroll=_ISSUE_UNROLL)


def _invert_slots(slot, n_slots):
    n_pairs = slot.shape[0]
    return pl.pallas_call(
        functools.partial(_invert_slots_kernel, n_pairs=n_pairs, n_slots=n_slots),
        out_shape=jax.ShapeDtypeStruct((n_slots,), jnp.int32),
        grid_spec=pltpu.PrefetchScalarGridSpec(
            num_scalar_prefetch=1,
            grid=(1,),
            in_specs=[],
            out_specs=pl.BlockSpec(memory_space=pltpu.SMEM),
        ),
        compiler_params=_params("arbitrary"),
        name="invert_slots",
    )(slot)


def _moe_layer(x, gain, w_router, wg, wu, wd, final_gain, *, tm, tf, hp=False):
    m, d = x.shape
    n_e = wg.shape[0]
    idx, wts = _router(x, gain, w_router)
    e_flat = idx[:, :2].reshape(-1)
    onehot = (e_flat[:, None] == jnp.arange(n_e, dtype=jnp.int32)[None, :]).astype(jnp.int32)
    csum = jnp.cumsum(onehot, axis=0)
    rank = jnp.sum(csum * onehot, axis=1) - 1
    tiles_e = (csum[-1] + tm - 1) // tm
    tile_end = jnp.cumsum(tiles_e)
    slot = ((tile_end - tiles_e)[e_flat] * tm + rank).astype(jnp.int32)
    n_tiles = (2 * m) // tm + n_e
    tile_expert = jnp.minimum(
        jnp.sum(jnp.arange(n_tiles, dtype=jnp.int32)[:, None] >= tile_end[None, :], axis=1), n_e - 1).astype(jnp.int32)
    n_active = tile_end[-1:].astype(jnp.int32)
    src_token = _invert_slots(slot, n_tiles * tm)
    ys = _moe_ffn(x, src_token, gain, wg, wu, wd, tile_expert, n_active, tm=tm, tf=tf, hp=hp)
    return _combine(x, wts, ys, slot, final_gain)


def _gdn_weights(w_in, a_log, dt_bias):
    h_n = GDN_HEADS
    main = 2 * h_n * GDN_DK + 2 * h_n * GDN_DV
    w_gate = jnp.pad(w_in[:, main:], ((0, 0), (0, LANES - 2 * h_n)))
    pad = (h_n, LANES - 2 * h_n)
    alog_row = jnp.pad(a_log.astype(F32), pad).reshape(1, LANES)
    dtb_row = jnp.pad(dt_bias.astype(F32), pad).reshape(1, LANES)
    return w_in[:, :main], w_gate, alog_row, dtb_row


def _gdn_layer_prompt(x, gain, w_in, conv_w, a_log, dt_bias, o_gain, w_out, *, batch, seq):
    conv_dim = GDN_HEADS * (2 * GDN_DK + GDN_DV)
    w_main, w_gate, alog_row, dtb_row = _gdn_weights(w_in, a_log, dt_bias)
    proj = _matmul(x, w_main, gain=gain, tm=512, name="gdn_in_proj")
    gates = _matmul(x, w_gate, gain=gain, hp=True, tm=512, name="gdn_gate_proj")
    o, s_fin = _gdn_prompt(proj, gates, conv_w, alog_row, dtb_row, o_gain, batch=batch, seq=seq)
    y = _matmul(o, w_out, res=x, tm=512, name="gdn_out_proj")
    conv_state = proj.reshape(batch, seq, -1)[:, seq - (CONV_W - 1):, :conv_dim]
    return y, conv_state, s_fin


def _gdn_layer_step(x, gain, conv_prev, s_prev, w_in, conv_w, a_log, dt_bias, o_gain, w_out):
    w_main, w_gate, alog_row, dtb_row = _gdn_weights(w_in, a_log, dt_bias)
    proj = _matmul(x, w_main, gain=gain, hp=True, tn=1024, name="gdn_in_proj_s")
    gates = _matmul(x, w_gate, gain=gain, hp=True, name="gdn_gate_proj_s")
    o, conv_new, s_new = _gdn_step(proj, gates, conv_prev, s_prev, conv_w, alog_row, dtb_row, o_gain)
    y = _matmul(o.reshape(x.shape[0], -1), w_out, res=x, hp=True, name="gdn_out_proj_s")
    return y, conv_new, s_new


def _moba_layer_prompt(x, gain, w_in, w_out, *, batch, seq):
    proj = _matmul(x, w_in, gain=gain, tm=512, name="att_in_proj")
    cos_t, sin_a, sin_b = _rope_tables(jnp.arange(seq, dtype=jnp.int32))
    o, k_rot, v = _moba_prompt(proj, cos_t, sin_a, sin_b, batch=batch, seq=seq)
    y = _matmul(o, w_out, res=x, tm=512, name="att_out_proj")
    return y, k_rot, v


def _moba_layer_step(x, gain, cache_k, cache_v, page_table, w_in, w_out):
    b = x.shape[0]
    hd = HEAD_DIM
    n_pool, page = cache_k.shape[0], cache_k.shape[1]
    past_len = page_table.shape[1] * page
    own = past_len // MOBA_BLOCK
    assert own * MOBA_BLOCK == past_len and own >= 1
    n_sel = min(MOBA_TOPK, own)
    ppb = MOBA_BLOCK // page
    proj = _matmul(x, w_in, gain=gain, hp=True, tn=1024, name="att_in_proj_s")
    ck = cache_k.reshape(n_pool, page * KV_HEADS, hd)
    cv = cache_v.reshape(n_pool, page * KV_HEADS, hd)
    kmeans = _cache_block_means(ck, page_table)
    cos_t, sin_a, sin_b = _rope_tables(jnp.full((1,), past_len, jnp.int32))
    q, k_rot, sel = _moba_select(proj, kmeans, cos_t, sin_a, sin_b, n_sel=n_sel)
    logical = sel[:, :, :n_sel, None] * ppb + jnp.arange(ppb, dtype=jnp.int32)
    pages = jnp.take_along_axis(page_table, logical.reshape(b, -1), axis=1).reshape(-1)
    v_new = proj[:, (ATT_HEADS + KV_HEADS) * hd:]
    o = _moba_step(q, k_rot.reshape(b, KV_HEADS, hd), v_new.reshape(b, KV_HEADS, hd), ck, cv, pages)
    y = _matmul(o, w_out, res=x, hp=True, name="att_out_proj_s")
    return y, k_rot.reshape(b, KV_HEADS * hd), v_new


def _norm_kernel(x_ref, g_ref, o_ref):
    o_ref[...] = _rms(x_ref[...], g_ref[...])


def _final_norm(x, gain, *, tm=512):
    m, d = x.shape
    tm = min(tm, m)
    return pl.pallas_call(
        _norm_kernel,
        out_shape=jax.ShapeDtypeStruct((m, d), F32),
        grid=(m // tm,),
        in_specs=[pl.BlockSpec((tm, d), lambda i: (i, 0)), pl.BlockSpec((1, d), lambda i: (0, 0))],
        out_specs=pl.BlockSpec((tm, d), lambda i: (i, 0)),
        compiler_params=_params("parallel"),
        name="final_norm",
    )(x, gain.reshape(1, d))


def kernel(x_prompt, x_sample, state_conv, state_delta, cache_k, cache_v, page_table, norm_mix, norm_ffn,
           norm_final, gdn_w_in, gdn_conv_w, gdn_a_log, gdn_dt_bias, gdn_o_norm, gdn_w_out, att_w_in,
           att_w_out, ffn_w_gate, ffn_w_up, ffn_w_down, moe_router, moe_w_gate, moe_w_up, moe_w_down):
    bp, seq, d = x_prompt.shape
    db, dec_seq, _ = x_sample.shape
    assert dec_seq == 1
    depth = norm_mix.shape[0]
    hp = x_prompt.reshape(bp * seq, d)
    hs = x_sample.reshape(db, d)
    conv_p, delta_p, k_p, v_p = [], [], [], []
    conv_s, delta_s, k_s, v_s = [], [], [], []
    for i in range(depth):
        j = i // 2
        last = i == depth - 1
        if i % 2 == 0:
            gdn = (gdn_w_in[j], gdn_conv_w[j], gdn_a_log[j], gdn_dt_bias[j], gdn_o_norm[j], gdn_w_out[j])
            hp, cp, sp = _gdn_layer_prompt(hp, norm_mix[i], *gdn, batch=bp, seq=seq)
            hs, cs, ss = _gdn_layer_step(hs, norm_mix[i], state_conv[j], state_delta[j], *gdn)
            conv_p.append(cp)
            delta_p.append(sp.astype(state_delta.dtype))
            conv_s.append(cs)
            delta_s.append(ss.astype(state_delta.dtype))
            ffn = (norm_ffn[i], ffn_w_gate[j], ffn_w_up[j], ffn_w_down[j])
            hp = _ffn(hp, *ffn, tf=896)
            hs = _ffn(hs, *ffn, hp=True)
            if last:
                hp = _final_norm(hp, norm_final)
                hs = _final_norm(hs, norm_final)
        else:
            hp, kp, vp = _moba_layer_prompt(hp, norm_mix[i], att_w_in[j], att_w_out[j], batch=bp, seq=seq)
            hs, kn, vn = _moba_layer_step(hs, norm_mix[i], cache_k[j], cache_v[j], page_table,
                                          att_w_in[j], att_w_out[j])
            k_p.append(kp.reshape(bp, seq, KV_HEADS, HEAD_DIM))
            v_p.append(vp.reshape(bp, seq, KV_HEADS, HEAD_DIM))
            k_s.append(kn.reshape(db, 1, KV_HEADS, HEAD_DIM))
            v_s.append(vn.reshape(db, 1, KV_HEADS, HEAD_DIM))
            moe = (norm_ffn[i], moe_router[j], moe_w_gate[j], moe_w_up[j], moe_w_down[j])
            fg = norm_final if last else None
            hp = _moe_layer(hp, *moe, fg, tm=512, tf=1792)
            hs = _moe_layer(hs, *moe, fg, tm=32, tf=1792 if last else 512, hp=not last)
    return (hp.reshape(bp, seq, d), hs.reshape(db, 1, d),
            jnp.stack(conv_p), jnp.stack(delta_p), jnp.stack(k_p), jnp.stack(v_p),
            jnp.stack(conv_s), jnp.stack(delta_s), jnp.stack(k_s), jnp.stack(v_s))
```

```python
import functools
import math

import jax
import jax.numpy as jnp
from jax import lax
from jax.experimental import pallas as pl
from jax.experimental.pallas import tpu as pltpu

F32 = jnp.float32
BF16 = jnp.bfloat16

GDN_HEADS = 8
GDN_DK = 128
GDN_DV = 128
CONV_W = 4
GDN_CHUNK = 64
ATT_HEADS = 8
KV_HEADS = 4
HEAD_DIM = 128
ROT_DIM = HEAD_DIM // 4
ROPE_THETA = 500000.0
MOBA_BLOCK = 256
MOBA_TOPK = 3
N_EXPERTS = 8
EPS = 1e-6

LANES = 128
SUBLANES = 8
VMEM_LIMIT = 56 * 1024 * 1024

NEG_INF = float("-inf")
_ISSUE_UNROLL = 8


def _params(*sem):
    return pltpu.CompilerParams(dimension_semantics=sem, vmem_limit_bytes=VMEM_LIMIT)


def _rms(x, gain):
    return x * lax.rsqrt(jnp.mean(x * x, axis=-1, keepdims=True) + EPS) * gain


def _silu(x):
    return x * jax.nn.sigmoid(x)


def _softplus(x):
    return jnp.maximum(x, 0.0) + jnp.log1p(jnp.exp(-jnp.abs(x)))


def _split2(a):
    hi = a.astype(BF16)
    lo = (a - hi.astype(F32)).astype(BF16)
    return hi, lo


def _split3(a):
    hi = a.astype(BF16)
    r = a - hi.astype(F32)
    mid = r.astype(BF16)
    lo = (r - mid.astype(F32)).astype(BF16)
    return hi, mid, lo


_NN = (((1,), (0,)), ((), ()))
_NT = (((1,), (1,)), ((), ()))
_TN = (((0,), (0,)), ((), ()))


def _dot(a, b, dims=_NN):
    return lax.dot_general(a, b, dims, preferred_element_type=F32)


def _dot_hp(a, b, dims=_NN):
    ah, al = _split2(a)
    bh, bl = _split2(b)
    return _dot(ah, bh, dims) + (_dot(al, bh, dims) + _dot(ah, bl, dims))


def _mm_kernel(*refs, has_gain, has_res, hp):
    it = iter(refs)
    a_ref = next(it)
    g_ref = next(it) if has_gain else None
    w_ref = next(it)
    r_ref = next(it) if has_res else None
    o_ref = next(it)
    an_ref = next(it)

    @pl.when(pl.program_id(1) == 0)
    def _():
        a = a_ref[...].astype(F32)
        if has_gain:
            a = _rms(a, g_ref[...])
        an_ref[...] = a.astype(an_ref.dtype)

    acc = (_dot_hp if hp else _dot)(an_ref[...], w_ref[...])
    if has_res:
        acc = acc + r_ref[...]
    o_ref[...] = acc.astype(o_ref.dtype)


def _matmul(a, w, *, gain=None, res=None, hp=False, tm=512, tn=None, out_dtype=F32, name="matmul"):
    m, k = a.shape
    n = w.shape[1]
    tm = min(tm, m)
    tn = n if tn is None else min(tn, n)
    assert m % tm == 0 and n % tn == 0
    args = [a]
    in_specs = [pl.BlockSpec((tm, k), lambda i, j: (i, 0))]
    if gain is not None:
        args.append(gain.reshape(1, k).astype(F32))
        in_specs.append(pl.BlockSpec((1, k), lambda i, j: (0, 0)))
    args.append(w if hp else w.astype(BF16))
    in_specs.append(pl.BlockSpec((k, tn), lambda i, j: (0, j)))
    if res is not None:
        args.append(res)
        in_specs.append(pl.BlockSpec((tm, tn), lambda i, j: (i, j)))
    scratch = [pltpu.VMEM((tm, k), F32 if hp else BF16)]
    return pl.pallas_call(
        functools.partial(_mm_kernel, has_gain=gain is not None, has_res=res is not None, hp=hp),
        out_shape=jax.ShapeDtypeStruct((m, n), out_dtype),
        grid=(m // tm, n // tn),
        in_specs=in_specs,
        out_specs=pl.BlockSpec((tm, tn), lambda i, j: (i, j)),
        scratch_shapes=scratch,
        compiler_params=_params("parallel", "arbitrary"),
        name=name,
    )(*args)


def _swiglu_step(xn, wg, wu, wd, hp):
    dot = _dot_hp if hp else _dot
    act = _silu(dot(xn, wg)) * dot(xn, wu)
    return dot(act if hp else act.astype(BF16), wd)


def _ffn_kernel(x_ref, g_ref, wg_ref, wu_ref, wd_ref, o_ref, xn_ref, acc_ref, *, hp):
    f = pl.program_id(1)

    @pl.when(f == 0)
    def _():
        xn_ref[...] = _rms(x_ref[...], g_ref[...]).astype(xn_ref.dtype)
        acc_ref[...] = jnp.zeros_like(acc_ref)

    acc_ref[...] += _swiglu_step(xn_ref[...], wg_ref[...], wu_ref[...], wd_ref[...], hp)

    @pl.when(f == pl.num_programs(1) - 1)
    def _():
        o_ref[...] = x_ref[...] + acc_ref[...]


def _ffn(x, gain, wg, wu, wd, *, hp=False, tm=1024, tf=512):
    m, d = x.shape
    dff = wg.shape[1]
    tm = min(tm, m)
    assert m % tm == 0 and dff % tf == 0
    wdt = F32 if hp else BF16
    return pl.pallas_call(
        functools.partial(_ffn_kernel, hp=hp),
        out_shape=jax.ShapeDtypeStruct((m, d), F32),
        grid=(m // tm, dff // tf),
        in_specs=[
            pl.BlockSpec((tm, d), lambda i, f: (i, 0)),
            pl.BlockSpec((1, d), lambda i, f: (0, 0)),
            pl.BlockSpec((d, tf), lambda i, f: (0, f)),
            pl.BlockSpec((d, tf), lambda i, f: (0, f)),
            pl.BlockSpec((tf, d), lambda i, f: (f, 0)),
        ],
        out_specs=pl.BlockSpec((tm, d), lambda i, f: (i, 0)),
        scratch_shapes=[pltpu.VMEM((tm, d), wdt), pltpu.VMEM((tm, d), F32)],
        compiler_params=_params("parallel", "arbitrary"),
        name="ffn",
    )(x, gain.reshape(1, d), wg.astype(wdt), wu.astype(wdt), wd.astype(wdt))


def _bdot(a, b, dims=_NN):
    return _dot(a.astype(BF16), b.astype(BF16), dims)


def _neumann_inverse_minus_eye(mats):
    c = mats[0].shape[0]
    ps = [-a for a in mats]
    ns = list(ps)
    for _ in range(int(math.log2(c)) - 1):
        ps = [_bdot(p, p) for p in ps]
        ns = [n + p + _bdot(n, p) for n, p in zip(ns, ps)]
    return ns


def _gdn_kernel(proj_ref, gate_ref, convw_ref, alog_ref, dtb_ref, ogain_ref,
                o_ref, sfin_ref, s_ref, tail_ref, *, tt, c):
    h_n, dk, dv = GDN_HEADS, GDN_DK, GDN_DV
    t = pl.program_id(1)

    @pl.when(t == 0)
    def _():
        s_ref[...] = jnp.zeros_like(s_ref)
        tail_ref[...] = jnp.zeros_like(tail_ref)

    gt = gate_ref[...]
    beta_all = jax.nn.sigmoid(gt)
    g_all = -jnp.exp(alog_ref[...]) * _softplus(gt + dtb_ref[...])

    ri = lax.broadcasted_iota(jnp.int32, (c, c), 0)
    ci = lax.broadcasted_iota(jnp.int32, (c, c), 1)
    tri = ri >= ci
    strict = ri > ci
    tri_b = tri.astype(BF16)
    ri2 = lax.broadcasted_iota(jnp.int32, (c, LANES), 0)
    ci2 = lax.broadcasted_iota(jnp.int32, (c, LANES), 1)
    upper_b = (ri2 <= ci2).astype(BF16)

    n_chunks = tt // c
    gams, gam_ts = [], []
    for ck in range(n_chunks):
        parts = _split3(g_all[ck * c:(ck + 1) * c, :])
        gam = sum(_dot(tri_b, p) for p in parts)
        gam_t = sum(_dot(p, upper_b, _TN) for p in parts)
        gams.append(gam)
        gam_ts.append(gam_t)

    def conv_silu(off):
        raw = proj_ref[:, off:off + LANES]
        xs = jnp.concatenate([tail_ref[:, off:off + LANES], raw], axis=0)
        w = convw_ref[:, off:off + LANES]
        acc = raw * w[CONV_W - 1:CONV_W]
        for j in range(CONV_W - 1):
            lo = SUBLANES - (CONV_W - 1) + j
            acc = acc + xs[lo:lo + tt] * w[j:j + 1]
        return _silu(acc)

    def l2n(x):
        return x * lax.rsqrt(jnp.sum(x * x, axis=-1, keepdims=True) + 1e-6)

    heads = range(h_n)
    pairs = [(ck, h) for ck in range(n_chunks) for h in heads]
    z_off = 2 * h_n * dk + h_n * dv
    q_all = [l2n(conv_silu(h * dk)) * (dk ** -0.5) for h in heads]
    k_all = [l2n(conv_silu(h_n * dk + h * dk)) for h in heads]
    v_all = [conv_silu(2 * h_n * dk + h * dv) for h in heads]

    def rows(x, ck):
        return x[ck * c:(ck + 1) * c]

    qc = [rows(q_all[h], ck) for ck, h in pairs]
    kc = [rows(k_all[h], ck) for ck, h in pairs]
    bcol = [rows(beta_all, ck)[:, h:h + 1] for ck, h in pairs]
    col = [gams[ck][:, h_n + h:h_n + h + 1] for ck, h in pairs]
    dm = [jnp.exp(jnp.where(tri, cl - gam_ts[ck][h_n + h:h_n + h + 1, :c], NEG_INF))
          for cl, (ck, h) in zip(col, pairs)]
    kb = [x * b for x, b in zip(kc, bcol)]
    kq = [_bdot(jnp.concatenate([x, y], axis=0), z, _NT) for x, y, z in zip(kb, qc, kc)]
    a = [jnp.where(strict, x[:c] * d, 0.0) for x, d in zip(kq, dm)]
    attn = [(x[c:] * d).astype(BF16) for x, d in zip(kq, dm)]
    ninv = _neumann_inverse_minus_eye(a)
    eg = [jnp.exp(cl) for cl in col]
    rhs = [jnp.concatenate([rows(v_all[h], ck) * b, x * e], axis=1)
           for (ck, h), b, x, e in zip(pairs, bcol, kb, eg)]
    sol = [r + _bdot(n, r) for n, r in zip(ninv, rhs)]
    glast = [cl[c - 1:c, :] for cl in col]
    lhs = [jnp.concatenate([x[:, dv:], y * e], axis=0).astype(BF16)
           for x, y, e in zip(sol, qc, eg)]
    k_dec = [(x * jnp.exp(g - cl)).astype(BF16) for x, g, cl in zip(kc, glast, col)]
    g_end = [jnp.exp(g) for g in glast]

    ogain = ogain_ref[...]
    for ck in range(n_chunks):
        idx = [ck * h_n + h for h in heads]
        s_old = [s_ref[h] for h in heads]
        both = [_dot(lhs[i], s.astype(BF16)) for i, s in zip(idx, s_old)]
        ub = [(sol[i][:, :dv] - b[:c]).astype(BF16) for i, b in zip(idx, both)]
        o = [b[c:] + _dot(attn[i], u) for i, b, u in zip(idx, both, ub)]
        for h, i, s, u in zip(heads, idx, s_old, ub):
            s_ref[h] = s * g_end[i] + _dot(k_dec[i], u, _TN)
        for h, x in zip(heads, o):
            zc = proj_ref[ck * c:(ck + 1) * c, z_off + h * dv:z_off + (h + 1) * dv]
            o_ref[ck * c:(ck + 1) * c, h * dv:(h + 1) * dv] = (_rms(x, ogain) * _silu(zc)).astype(o_ref.dtype)

    tail_ref[...] = proj_ref[tt - SUBLANES:tt, :tail_ref.shape[1]]

    @pl.when(t == pl.num_programs(1) - 1)
    def _():
        sfin_ref[0] = s_ref[...]


def _gdn_prompt(proj, gates, conv_w, alog_row, dtb_row, o_gain, *, batch, seq, tt=256):
    h_n, dk, dv = GDN_HEADS, GDN_DK, GDN_DV
    conv_dim = h_n * (2 * dk + dv)
    width = proj.shape[1]
    tt = min(tt, seq)
    c = math.gcd(seq, GDN_CHUNK)
    assert seq % tt == 0 and tt % c == 0 and tt >= SUBLANES
    nt = seq // tt
    return pl.pallas_call(
        functools.partial(_gdn_kernel, tt=tt, c=c),
        out_shape=(jax.ShapeDtypeStruct((batch * seq, h_n * dv), BF16),
                   jax.ShapeDtypeStruct((batch, h_n, dk, dv), F32)),
        grid=(batch, nt),
        in_specs=[
            pl.BlockSpec((tt, width), lambda b, t: (b * nt + t, 0)),
            pl.BlockSpec((tt, LANES), lambda b, t: (b * nt + t, 0)),
            pl.BlockSpec((CONV_W, conv_dim), lambda b, t: (0, 0)),
            pl.BlockSpec((1, LANES), lambda b, t: (0, 0)),
            pl.BlockSpec((1, LANES), lambda b, t: (0, 0)),
            pl.BlockSpec((1, dv), lambda b, t: (0, 0)),
        ],
        out_specs=(pl.BlockSpec((tt, h_n * dv), lambda b, t: (b * nt + t, 0)),
                   pl.BlockSpec((1, h_n, dk, dv), lambda b, t: (b, 0, 0, 0))),
        scratch_shapes=[pltpu.VMEM((h_n, dk, dv), F32), pltpu.VMEM((SUBLANES, conv_dim), F32)],
        compiler_params=_params("parallel", "arbitrary"),
        name="gdn_prompt",
    )(proj, gates, conv_w, alog_row, dtb_row, o_gain.reshape(1, dv))


def _gdn_step_kernel(proj_ref, gate_ref, cprev_ref, s_ref, convw_ref, alog_ref, dtb_ref, ogain_ref,
                     o_ref, cnew_ref, snew_ref):
    h_n, dk, dv = GDN_HEADS, GDN_DK, GDN_DV
    conv_dim = h_n * (2 * dk + dv)
    u_row = proj_ref[0]
    prev = cprev_ref[0]
    w = convw_ref[...]
    raw = u_row[:, :conv_dim]
    acc = raw * w[CONV_W - 1:CONV_W]
    for j in range(CONV_W - 1):
        acc = acc + prev[j:j + 1] * w[j:j + 1]
    qkv = _silu(acc)
    for j in range(CONV_W - 2):
        cnew_ref[0, j:j + 1, :] = prev[j + 1:j + 2]
    cnew_ref[0, CONV_W - 2:CONV_W - 1, :] = raw

    gt = gate_ref[0]
    beta_all = jax.nn.sigmoid(gt)
    g_all = -jnp.exp(alog_ref[...]) * _softplus(gt + dtb_ref[...])
    rows = lax.broadcasted_iota(jnp.int32, (SUBLANES, LANES), 0)
    ogain = ogain_ref[...]

    def l2n(x):
        return x * lax.rsqrt(jnp.sum(x * x, axis=-1, keepdims=True) + 1e-6)

    def as_row0(x, y=None):
        out = jnp.where(rows == 0, jnp.broadcast_to(x, (SUBLANES, LANES)), 0.0)
        if y is not None:
            out = jnp.where(rows == 1, jnp.broadcast_to(y, (SUBLANES, LANES)), out)
        return out

    for h in range(h_n):
        q = l2n(qkv[:, h * dk:(h + 1) * dk]) * (dk ** -0.5)
        k = l2n(qkv[:, h_n * dk + h * dk:h_n * dk + (h + 1) * dk])
        v = qkv[:, 2 * h_n * dk + h * dv:2 * h_n * dk + (h + 1) * dv]
        z = u_row[:, conv_dim + h * dv:conv_dim + (h + 1) * dv]
        beta = beta_all[:, h:h + 1]
        eg = jnp.exp(g_all[:, h_n + h:h_n + h + 1])
        s = s_ref[0, h]
        kcd = k * (beta * eg)
        q_dec = q * eg
        both = _dot_hp(as_row0(kcd, q_dec), s)
        u = v * beta - both[0:1]
        o = both[1:2] + jnp.sum(q * k, axis=-1, keepdims=True) * u
        snew_ref[0, h] = s * eg + _dot_hp(as_row0(k), as_row0(u), _TN)
        o_ref[0, :, h * dv:(h + 1) * dv] = (_rms(o, ogain) * _silu(z)).astype(o_ref.dtype)


def _gdn_step(proj, gates, conv_prev, s_prev, conv_w, alog_row, dtb_row, o_gain):
    h_n, dk, dv = GDN_HEADS, GDN_DK, GDN_DV
    conv_dim = h_n * (2 * dk + dv)
    b, width = proj.shape
    return pl.pallas_call(
        _gdn_step_kernel,
        out_shape=(jax.ShapeDtypeStruct((b, 1, h_n * dv), F32),
                   jax.ShapeDtypeStruct((b, CONV_W - 1, conv_dim), F32),
                   jax.ShapeDtypeStruct((b, h_n, dk, dv), F32)),
        grid=(b,),
        in_specs=[
            pl.BlockSpec((1, 1, width), lambda i: (i, 0, 0)),
            pl.BlockSpec((1, 1, LANES), lambda i: (i, 0, 0)),
            pl.BlockSpec((1, CONV_W - 1, conv_dim), lambda i: (i, 0, 0)),
            pl.BlockSpec((1, h_n, dk, dv), lambda i: (i, 0, 0, 0)),
            pl.BlockSpec((CONV_W, conv_dim), lambda i: (0, 0)),
            pl.BlockSpec((1, LANES), lambda i: (0, 0)),
            pl.BlockSpec((1, LANES), lambda i: (0, 0)),
            pl.BlockSpec((1, dv), lambda i: (0, 0)),
        ],
        out_specs=(pl.BlockSpec((1, 1, h_n * dv), lambda i: (i, 0, 0)),
                   pl.BlockSpec((1, CONV_W - 1, conv_dim), lambda i: (i, 0, 0)),
                   pl.BlockSpec((1, h_n, dk, dv), lambda i: (i, 0, 0, 0))),
        compiler_params=_params("parallel"),
        name="gdn_step",
    )(proj.reshape(b, 1, width), gates.reshape(b, 1, LANES), conv_prev, s_prev, conv_w,
      alog_row, dtb_row, o_gain.reshape(1, dv))


def _rope(x, cos, sin_a, sin_b):
    half = ROT_DIM // 2
    return (x * cos + pltpu.roll(x, HEAD_DIM - half, 1) * sin_a + pltpu.roll(x, half, 1) * sin_b)


def _rope_tables(pos):
    half = ROT_DIM // 2
    inv = jnp.power(ROPE_THETA, -jnp.arange(half, dtype=F32) * (2.0 / ROT_DIM))
    ang = pos.astype(F32)[:, None] * inv[None, :]
    cos, sin = jnp.cos(ang), jnp.sin(ang)
    n = pos.shape[0]
    rest = HEAD_DIM - ROT_DIM
    cos_t = jnp.concatenate([cos, cos, jnp.ones((n, rest), F32)], axis=1)
    sin_a = jnp.concatenate([-sin, jnp.zeros((n, half + rest), F32)], axis=1)
    sin_b = jnp.concatenate([jnp.zeros((n, half), F32), sin, jnp.zeros((n, rest), F32)], axis=1)
    return cos_t, sin_a, sin_b


def _moba_kernel(q_ref, k_ref, v_ref, cos_ref, sa_ref, sb_ref, o_ref, kf_ref, vf_ref,
                 kout_ref, kb_ref, vb_ref, *, seq):
    blk, hd = MOBA_BLOCK, HEAD_DIM
    group = ATT_HEADS // KV_HEADS
    nb = seq // blk
    nq = group * blk
    scale = hd ** -0.5

    kout_ref[...] = _rope(k_ref[...], cos_ref[...], sa_ref[...], sb_ref[...])
    kb_ref[...] = kout_ref[...].astype(BF16)
    vb_ref[...] = v_ref[...].astype(BF16)
    for gg in range(KV_HEADS):
        @pl.when(pl.program_id(1) == gg)
        def _(gg=gg):
            kf_ref[pl.ds(gg, seq, stride=KV_HEADS), :] = kout_ref[...]
            vf_ref[pl.ds(gg, seq, stride=KV_HEADS), :] = v_ref[...]

    kmean = jnp.concatenate(
        [jnp.sum(kout_ref[n * blk:(n + 1) * blk, :], axis=0, keepdims=True) for n in range(nb)],
        axis=0) * (1.0 / blk)

    blk_id = lax.broadcasted_iota(jnp.int32, (nb, nq), 0)
    key_r = lax.broadcasted_iota(jnp.int32, (blk, nq), 0)
    qry_c = lax.broadcasted_iota(jnp.int32, (blk, nq), 1) % blk
    causal = key_r <= qry_c

    for i in range(nb):
        rows = slice(i * blk, (i + 1) * blk)
        cos, sa, sb = cos_ref[rows, :], sa_ref[rows, :], sb_ref[rows, :]
        qs = [_rope(q_ref[rows, g * hd:(g + 1) * hd], cos, sa, sb) for g in range(group)]
        q = jnp.concatenate(qs, axis=0)
        qb = (q * (scale * math.log2(math.e))).astype(BF16)

        if i > MOBA_TOPK:
            gate = jnp.where(blk_id < i, _dot_hp(kmean, q, _NT), NEG_INF)
            rank = jnp.zeros((nb, nq), F32)
            for m in range(i):
                gm = gate[m:m + 1, :]
                rank = rank + jnp.where((gm > gate) | ((gm == gate) & (m < blk_id)), 1.0, 0.0)
            sel = jnp.where((blk_id < i) & (rank < MOBA_TOPK), 1.0, 0.0)
        else:
            sel = None

        ss = []
        for j in range(i + 1):
            s = _dot(kb_ref[j * blk:(j + 1) * blk, :], qb, _NT)
            if j == i:
                s = jnp.where(causal, s, NEG_INF)
            elif sel is not None:
                s = jnp.where(sel[j:j + 1, :] > 0.0, s, NEG_INF)
            ss.append(s)
        m_all = functools.reduce(jnp.maximum, [jnp.max(s, axis=0, keepdims=True) for s in ss])
        ps = [jnp.exp2(s - m_all) for s in ss]
        l_all = functools.reduce(jnp.add, [jnp.sum(p, axis=0, keepdims=True) for p in ps])
        acc = functools.reduce(
            jnp.add, [_dot(vb_ref[j * blk:(j + 1) * blk, :], p.astype(BF16), _TN) for j, p in enumerate(ps)])
        out = (acc / l_all).T
        for g in range(group):
            o_ref[rows, g * hd:(g + 1) * hd] = out[g * blk:(g + 1) * blk].astype(o_ref.dtype)


def _moba_prompt(proj, cos_t, sin_a, sin_b, *, batch, seq):
    hd = HEAD_DIM
    group = ATT_HEADS // KV_HEADS
    assert seq % MOBA_BLOCK == 0
    tbl = pl.BlockSpec((seq, hd), lambda b, g: (0, 0))
    kv_rows = jax.ShapeDtypeStruct((batch * seq * KV_HEADS, hd), F32)
    kv_spec = pl.BlockSpec((seq * KV_HEADS, hd), lambda b, g: (b, 0))
    return pl.pallas_call(
        functools.partial(_moba_kernel, seq=seq),
        out_shape=(jax.ShapeDtypeStruct((batch * seq, ATT_HEADS * hd), BF16), kv_rows, kv_rows),
        grid=(batch, KV_HEADS),
        in_specs=[
            pl.BlockSpec((seq, group * hd), lambda b, g: (b, g)),
            pl.BlockSpec((seq, hd), lambda b, g: (b, ATT_HEADS + g)),
            pl.BlockSpec((seq, hd), lambda b, g: (b, ATT_HEADS + KV_HEADS + g)),
            tbl, tbl, tbl,
        ],
        out_specs=(pl.BlockSpec((seq, group * hd), lambda b, g: (b, g)), kv_spec, kv_spec),
        scratch_shapes=[pltpu.VMEM((seq, hd), F32), pltpu.VMEM((seq, hd), BF16), pltpu.VMEM((seq, hd), BF16)],
        compiler_params=_params("parallel", "arbitrary"),
        name="moba_prompt",
    )(proj, proj, proj, cos_t, sin_a, sin_b)


def _kmean_kernel(pt_ref, *refs, pages, pages_per_block):
    page_refs, o_ref = refs[:pages], refs[pages]
    rows = page_refs[0].shape[1]
    tokens = pages_per_block * rows // KV_HEADS
    for n in range(pages // pages_per_block):
        part = jnp.zeros((SUBLANES, HEAD_DIM), F32)
        for p in range(pages_per_block):
            x = page_refs[n * pages_per_block + p][0]
            part = part + jnp.sum(x.reshape(rows // SUBLANES, SUBLANES, HEAD_DIM), axis=0)
        acc = part[:KV_HEADS]
        for r in range(1, SUBLANES // KV_HEADS):
            acc = acc + part[r * KV_HEADS:(r + 1) * KV_HEADS]
        o_ref[0, n] = acc * (1.0 / tokens)


def _cache_block_means(cache, page_table, *, blocks_per_step=8):
    n_pool, rows, hd = cache.shape
    page = rows // KV_HEADS
    b, n_pages = page_table.shape
    ppb = MOBA_BLOCK // page
    assert ppb * page == MOBA_BLOCK and n_pages % ppb == 0 and rows % SUBLANES == 0
    n_blocks = n_pages // ppb
    bps = math.gcd(blocks_per_step, n_blocks)
    pages = bps * ppb
    specs = [pl.BlockSpec((1, rows, hd), lambda i, s, pt, p=p: (pt[i, s * pages + p], 0, 0))
             for p in range(pages)]
    return pl.pallas_call(
        functools.partial(_kmean_kernel, pages=pages, pages_per_block=ppb),
        out_shape=jax.ShapeDtypeStruct((b, n_blocks, KV_HEADS, hd), F32),
        grid_spec=pltpu.PrefetchScalarGridSpec(
            num_scalar_prefetch=1,
            grid=(b, n_blocks // bps),
            in_specs=specs,
            out_specs=pl.BlockSpec((1, bps, KV_HEADS, hd), lambda i, s, pt: (i, s, 0, 0)),
        ),
        compiler_params=_params("parallel", "arbitrary"),
        name="cache_block_means",
    )(page_table, *([cache] * pages))


def _moba_select_kernel(proj_ref, km_ref, cos_ref, sa_ref, sb_ref, q_ref, k_ref, sel_ref, *, n_sel):
    hd = HEAD_DIM
    group = ATT_HEADS // KV_HEADS
    row = proj_ref[0]
    cos, sa, sb = cos_ref[...], sa_ref[...], sb_ref[...]
    rows = lax.broadcasted_iota(jnp.int32, (SUBLANES, hd), 0)
    q = jnp.zeros((SUBLANES, hd), F32)
    for h in range(ATT_HEADS):
        qh = _rope(row[:, h * hd:(h + 1) * hd], cos, sa, sb)
        q = jnp.where(rows == h, jnp.broadcast_to(qh, (SUBLANES, hd)), q)
    q_ref[0] = q
    for g in range(KV_HEADS):
        off = (ATT_HEADS + g) * hd
        k_ref[0, :, g * hd:(g + 1) * hd] = _rope(row[:, off:off + hd], cos, sa, sb)

    nb = km_ref.shape[1]
    hrow = lax.broadcasted_iota(jnp.int32, (ATT_HEADS, nb), 0)
    lane = lax.broadcasted_iota(jnp.int32, (ATT_HEADS, nb), 1)
    gate = jnp.zeros((ATT_HEADS, nb), F32)
    for g in range(KV_HEADS):
        gg = _dot_hp(q, km_ref[0, :, g, :], _NT)
        gate = jnp.where(hrow // group == g, gg, gate)
    out_lane = lax.broadcasted_iota(jnp.int32, (ATT_HEADS, LANES), 1)
    sel = jnp.zeros((ATT_HEADS, LANES), jnp.int32)
    for t in range(n_sel):
        m = jnp.max(gate, axis=-1, keepdims=True)
        idx = jnp.min(jnp.where(gate == m, lane, nb), axis=-1, keepdims=True)
        sel = jnp.where(out_lane == t, idx, sel)
        gate = jnp.where(lane == idx, NEG_INF, gate)
    sel_ref[0] = sel


def _moba_select(proj, kmeans, cos_t, sin_a, sin_b, *, n_sel):
    b, width = proj.shape
    nb = kmeans.shape[1]
    hd = HEAD_DIM
    tbl = pl.BlockSpec((1, hd), lambda i: (0, 0))
    return pl.pallas_call(
        functools.partial(_moba_select_kernel, n_sel=n_sel),
        out_shape=(jax.ShapeDtypeStruct((b, ATT_HEADS, hd), F32),
                   jax.ShapeDtypeStruct((b, 1, KV_HEADS * hd), F32),
                   jax.ShapeDtypeStruct((b, ATT_HEADS, LANES), jnp.int32)),
        grid=(b,),
        in_specs=[pl.BlockSpec((1, 1, width), lambda i: (i, 0, 0)),
                  pl.BlockSpec((1, nb, KV_HEADS, hd), lambda i: (i, 0, 0, 0)),
                  tbl, tbl, tbl],
        out_specs=(pl.BlockSpec((1, ATT_HEADS, hd), lambda i: (i, 0, 0)),
                   pl.BlockSpec((1, 1, KV_HEADS * hd), lambda i: (i, 0, 0)),
                   pl.BlockSpec((1, ATT_HEADS, LANES), lambda i: (i, 0, 0))),
        compiler_params=_params("parallel"),
        name="moba_select",
    )(proj.reshape(b, 1, width), kmeans, cos_t, sin_a, sin_b)


def _moba_step_kernel(pg_ref, q_ref, kn_ref, vn_ref, *refs, n_pages, heads):
    n = heads * n_pages
    k_refs, v_refs, o_ref = refs[:n], refs[n:2 * n], refs[2 * n]
    hd = HEAD_DIM
    group = ATT_HEADS // KV_HEADS
    scale = hd ** -0.5
    page = k_refs[0].shape[1] // KV_HEADS

    for hh in range(heads):
        h = pl.program_id(1) * heads + hh
        g = h // group

        def head_rows(ref, g=g):
            out = ref[0, pl.ds(0, page, stride=KV_HEADS), :]
            for gg in range(1, KV_HEADS):
                out = jnp.where(g == gg, ref[0, pl.ds(gg, page, stride=KV_HEADS), :], out)
            return out

        q = q_ref[0, pl.ds(h, 1), :]
        q8 = jnp.broadcast_to(q, (SUBLANES, hd))
        kn = kn_ref[0, pl.ds(g, 1), :]
        vn = vn_ref[0, pl.ds(g, 1), :]
        s_self = jnp.sum(q * kn, axis=-1, keepdims=True) * scale
        scores = [_dot_hp(q8, head_rows(k_refs[hh * n_pages + p]), _NT)[0:1] * scale
                  for p in range(n_pages)]
        m = functools.reduce(jnp.maximum, [jnp.max(s, axis=-1, keepdims=True) for s in scores], s_self)
        p_self = jnp.exp(s_self - m)
        l = p_self
        acc = p_self * vn
        for p in range(n_pages):
            e = jnp.exp(scores[p] - m)
            l = l + jnp.sum(e, axis=-1, keepdims=True)
            acc = acc + _dot_hp(jnp.broadcast_to(e, (SUBLANES, page)), head_rows(v_refs[hh * n_pages + p]))[0:1]
        o_ref[0, hh:hh + 1, :] = (acc / l).astype(o_ref.dtype)


def _moba_step(q, k_new, v_new, cache_k, cache_v, pages, *, heads=4):
    b = q.shape[0]
    hd = HEAD_DIM
    rows = cache_k.shape[1]
    n_pages = pages.shape[0] // (b * ATT_HEADS)
    assert ATT_HEADS % heads == 0
    steps = ATT_HEADS // heads

    def page_spec(k):
        return pl.BlockSpec((1, rows, hd),
                            lambda i, s, pg, k=k: (pg[(i * ATT_HEADS + s * heads) * n_pages + k], 0, 0))

    specs = ([pl.BlockSpec((1, ATT_HEADS, hd), lambda i, s, pg: (i, 0, 0)),
              pl.BlockSpec((1, KV_HEADS, hd), lambda i, s, pg: (i, 0, 0)),
              pl.BlockSpec((1, KV_HEADS, hd), lambda i, s, pg: (i, 0, 0))]
             + [page_spec(k) for k in range(heads * n_pages)] * 2)
    out = pl.pallas_call(
        functools.partial(_moba_step_kernel, n_pages=n_pages, heads=heads),
        out_shape=jax.ShapeDtypeStruct((b * steps, heads, hd), F32),
        grid_spec=pltpu.PrefetchScalarGridSpec(
            num_scalar_prefetch=1,
            grid=(b, steps),
            in_specs=specs,
            out_specs=pl.BlockSpec((1, heads, hd), lambda i, s, pg: (i * steps + s, 0, 0)),
        ),
        compiler_params=_params("parallel", "arbitrary"),
        name="moba_step",
    )(pages, q, k_new, v_new, *([cache_k] * (heads * n_pages)), *([cache_v] * (heads * n_pages)))
    return out.reshape(b, ATT_HEADS * hd)


def _router_kernel(x_ref, g_ref, w_ref, idx_ref, wts_ref):
    logits = _dot_hp(_rms(x_ref[...], g_ref[...]), w_ref[...])
    lane = lax.broadcasted_iota(jnp.int32, logits.shape, 1)
    logits = jnp.where(lane < N_EXPERTS, logits, NEG_INF)
    m1 = jnp.max(logits, axis=-1, keepdims=True)
    i1 = jnp.min(jnp.where(logits == m1, lane, LANES), axis=-1, keepdims=True)
    rest = jnp.where(lane == i1, NEG_INF, logits)
    m2 = jnp.max(rest, axis=-1, keepdims=True)
    i2 = jnp.min(jnp.where(rest == m2, lane, LANES), axis=-1, keepdims=True)
    e = jnp.exp(m2 - m1)
    w1 = 1.0 / (1.0 + e)
    idx_ref[...] = jnp.where(lane == 0, i1, jnp.where(lane == 1, i2, 0))
    wts_ref[...] = jnp.where(lane == 0, w1, jnp.where(lane == 1, e * w1, 0.0))


def _router(x, gain, w_router, *, tm=512):
    m, d = x.shape
    tm = min(tm, m)
    assert m % tm == 0
    w = jnp.pad(w_router, ((0, 0), (0, LANES - w_router.shape[1])))
    return pl.pallas_call(
        _router_kernel,
        out_shape=(jax.ShapeDtypeStruct((m, LANES), jnp.int32), jax.ShapeDtypeStruct((m, LANES), F32)),
        grid=(m // tm,),
        in_specs=[pl.BlockSpec((tm, d), lambda i: (i, 0)), pl.BlockSpec((1, d), lambda i: (0, 0)),
                  pl.BlockSpec((d, LANES), lambda i: (0, 0))],
        out_specs=(pl.BlockSpec((tm, LANES), lambda i: (i, 0)), pl.BlockSpec((tm, LANES), lambda i: (i, 0))),
        compiler_params=_params("parallel"),
        name="router",
    )(x, gain.reshape(1, d), w)


def _row_copy(src, dst, sem, i, j):
    return pltpu.make_async_copy(src.at[pl.ds(i, 1)], dst.at[pl.ds(j, 1)], sem)


def _moe_ffn_kernel(te_ref, na_ref, src_ref, x_ref, g_ref, wg_ref, wu_ref, wd_ref, o_ref,
                    xbuf_ref, xn_ref, acc_ref, sem, *, tm, hp):
    t, f = pl.program_id(0), pl.program_id(1)
    last = f == pl.num_programs(1) - 1
    n_active = na_ref[0]
    active = t < n_active
    slot = t % 2

    def start_gather(tile, dst_slot):
        def issue(r, carry):
            _row_copy(x_ref, xbuf_ref.at[dst_slot], sem.at[dst_slot], src_ref[tile * tm + r], r).start()
            return carry

        lax.fori_loop(0, tm, issue, 0, unroll=_ISSUE_UNROLL)

    @pl.when(active)
    def _():
        @pl.when(f == 0)
        def _():
            @pl.when(t == 0)
            def _():
                start_gather(0, 0)

            pltpu.make_async_copy(x_ref.at[pl.ds(0, tm)], xbuf_ref.at[slot], sem.at[slot]).wait()
            xn_ref[...] = _rms(xbuf_ref[slot], g_ref[...]).astype(xn_ref.dtype)
            acc_ref[...] = jnp.zeros_like(acc_ref)

            @pl.when(t + 1 < n_active)
            def _():
                start_gather(t + 1, 1 - slot)

        acc_ref[...] += _swiglu_step(xn_ref[...], wg_ref[0], wu_ref[0], wd_ref[0], hp)

        @pl.when(last)
        def _():
            o_ref[...] = acc_ref[...]

    @pl.when(jnp.logical_and(jnp.logical_not(active), last))
    def _():
        o_ref[...] = jnp.zeros_like(o_ref)


def _moe_ffn(x, src_token, gain, wg, wu, wd, tile_expert, n_active, *, tm, tf, hp=False):
    d = x.shape[1]
    s = src_token.shape[0]
    dff = wg.shape[2]
    assert s % tm == 0 and dff % tf == 0
    wdt = F32 if hp else BF16
    return pl.pallas_call(
        functools.partial(_moe_ffn_kernel, tm=tm, hp=hp),
        out_shape=jax.ShapeDtypeStruct((s, d), F32),
        grid_spec=pltpu.PrefetchScalarGridSpec(
            num_scalar_prefetch=3,
            grid=(s // tm, dff // tf),
            in_specs=[
                pl.BlockSpec(memory_space=pl.ANY),
                pl.BlockSpec((1, d), lambda t, f, te, na, src: (0, 0)),
                pl.BlockSpec((1, d, tf), lambda t, f, te, na, src: (te[t], 0, f)),
                pl.BlockSpec((1, d, tf), lambda t, f, te, na, src: (te[t], 0, f)),
                pl.BlockSpec((1, tf, d), lambda t, f, te, na, src: (te[t], f, 0)),
            ],
            out_specs=pl.BlockSpec((tm, d), lambda t, f, te, na, src: (t, 0)),
            scratch_shapes=[pltpu.VMEM((2, tm, d), F32), pltpu.VMEM((tm, d), wdt), pltpu.VMEM((tm, d), F32),
                            pltpu.SemaphoreType.DMA((2,))],
        ),
        compiler_params=_params("arbitrary", "arbitrary"),
        name="moe_ffn",
    )(tile_expert, n_active, src_token, x, gain.reshape(1, d), wg.astype(wdt), wu.astype(wdt), wd.astype(wdt))


def _combine_kernel(slot_ref, x_ref, wts_ref, *refs, tc, has_norm):
    if has_norm:
        gain_ref, y_ref, o_ref, buf_ref, sem = refs
    else:
        y_ref, o_ref, buf_ref, sem = refs
    i = pl.program_id(0)
    slot = i % 2

    def start_gather(step, dst_slot):
        def issue(r, carry):
            for k in range(2):
                _row_copy(y_ref, buf_ref.at[dst_slot, k], sem.at[dst_slot], slot_ref[2 * (step * tc + r) + k], r).start()
            return carry

        lax.fori_loop(0, tc, issue, 0, unroll=_ISSUE_UNROLL)

    @pl.when(i == 0)
    def _():
        start_gather(0, 0)

    @pl.when(i + 1 < pl.num_programs(0))
    def _():
        start_gather(i + 1, 1 - slot)

    for k in range(2):
        pltpu.make_async_copy(y_ref.at[pl.ds(0, tc)], buf_ref.at[slot, k], sem.at[slot]).wait()
    w = wts_ref[...]
    out = x_ref[...] + (w[:, 0:1] * buf_ref[slot, 0] + w[:, 1:2] * buf_ref[slot, 1])
    if has_norm:
        out = _rms(out, gain_ref[...])
    o_ref[...] = out


def _combine(x, wts, y_sorted, slots, final_gain, *, tc=256):
    m, d = x.shape
    tc = min(tc, m)
    assert m % tc == 0
    has_norm = final_gain is not None
    args = [slots, x, wts]
    in_specs = [pl.BlockSpec((tc, d), lambda i, sl: (i, 0)), pl.BlockSpec((tc, LANES), lambda i, sl: (i, 0))]
    if has_norm:
        args.append(final_gain.reshape(1, d))
        in_specs.append(pl.BlockSpec((1, d), lambda i, sl: (0, 0)))
    args.append(y_sorted)
    in_specs.append(pl.BlockSpec(memory_space=pl.ANY))
    return pl.pallas_call(
        functools.partial(_combine_kernel, tc=tc, has_norm=has_norm),
        out_shape=jax.ShapeDtypeStruct((m, d), F32),
        grid_spec=pltpu.PrefetchScalarGridSpec(
            num_scalar_prefetch=1,
            grid=(m // tc,),
            in_specs=in_specs,
            out_specs=pl.BlockSpec((tc, d), lambda i, sl: (i, 0)),
            scratch_shapes=[pltpu.VMEM((2, 2, tc, d), F32), pltpu.SemaphoreType.DMA((2,))],
        ),
        compiler_params=_params("arbitrary"),
        name="moe_combine",
    )(*args)


def _invert_slots_kernel(slot_ref, src_ref, *, n_pairs, n_slots):
    def clear(s, carry):
        src_ref[s] = 0
        return carry

    def put(tok, carry):
        src_ref[slot_ref[2 * tok]] = tok
        src_ref[slot_ref[2 * tok + 1]] = tok
        return carry

    lax.fori_loop(0, n_slots, clear, 0, unroll=_ISSUE_UNROLL)
    lax.fori_loop(0, n_pairs // 2, put, 0, unroll=_ISSUE_UNROLL)


def _invert_slots(slot, n_slots):
    n_pairs = slot.shape[0]
    return pl.pallas_call(
        functools.partial(_invert_slots_kernel, n_pairs=n_pairs, n_slots=n_slots),
        out_shape=jax.ShapeDtypeStruct((n_slots,), jnp.int32),
        grid_spec=pltpu.PrefetchScalarGridSpec(
            num_scalar_prefetch=1,
            grid=(1,),
            in_specs=[],
            out_specs=pl.BlockSpec(memory_space=pltpu.SMEM),
        ),
        compiler_params=_params("arbitrary"),
        name="invert_slots",
    )(slot)


def _moe_layer(x, gain, w_router, wg, wu, wd, final_gain, *, tm, tf, hp=False):
    m, d = x.shape
    n_e = wg.shape[0]
    idx, wts = _router(x, gain, w_router)
    e_flat = idx[:, :2].reshape(-1)
    onehot = (e_flat[:, None] == jnp.arange(n_e, dtype=jnp.int32)[None, :]).astype(jnp.int32)
    csum = jnp.cumsum(onehot, axis=0)
    rank = jnp.sum(csum * onehot, axis=1) - 1
    tiles_e = (csum[-1] + tm - 1) // tm
    tile_end = jnp.cumsum(tiles_e)
    slot = ((tile_end - tiles_e)[e_flat] * tm + rank).astype(jnp.int32)
    n_tiles = (2 * m) // tm + n_e
    tile_expert = jnp.minimum(
        jnp.sum(jnp.arange(n_tiles, dtype=jnp.int32)[:, None] >= tile_end[None, :], axis=1), n_e - 1).astype(jnp.int32)
    n_active = tile_end[-1:].astype(jnp.int32)
    src_token = _invert_slots(slot, n_tiles * tm)
    ys = _moe_ffn(x, src_token, gain, wg, wu, wd, tile_expert, n_active, tm=tm, tf=tf, hp=hp)
    return _combine(x, wts, ys, slot, final_gain)


def _gdn_weights(w_in, a_log, dt_bias):
    h_n = GDN_HEADS
    main = 2 * h_n * GDN_DK + 2 * h_n * GDN_DV
    w_gate = jnp.pad(w_in[:, main:], ((0, 0), (0, LANES - 2 * h_n)))
    pad = (h_n, LANES - 2 * h_n)
    alog_row = jnp.pad(a_log.astype(F32), pad).reshape(1, LANES)
    dtb_row = jnp.pad(dt_bias.astype(F32), pad).reshape(1, LANES)
    return w_in[:, :main], w_gate, alog_row, dtb_row


def _gdn_layer_prompt(x, gain, w_in, conv_w, a_log, dt_bias, o_gain, w_out, *, batch, seq):
    conv_dim = GDN_HEADS * (2 * GDN_DK + GDN_DV)
    w_main, w_gate, alog_row, dtb_row = _gdn_weights(w_in, a_log, dt_bias)
    proj = _matmul(x, w_main, gain=gain, tm=512, name="gdn_in_proj")
    gates = _matmul(x, w_gate, gain=gain, hp=True, tm=512, name="gdn_gate_proj")
    o, s_fin = _gdn_prompt(proj, gates, conv_w, alog_row, dtb_row, o_gain, batch=batch, seq=seq)
    y = _matmul(o, w_out, res=x, tm=512, name="gdn_out_proj")
    conv_state = proj.reshape(batch, seq, -1)[:, seq - (CONV_W - 1):, :conv_dim]
    return y, conv_state, s_fin


def _gdn_layer_step(x, gain, conv_prev, s_prev, w_in, conv_w, a_log, dt_bias, o_gain, w_out):
    w_main, w_gate, alog_row, dtb_row = _gdn_weights(w_in, a_log, dt_bias)
    proj = _matmul(x, w_main, gain=gain, hp=True, tn=1024, name="gdn_in_proj_s")
    gates = _matmul(x, w_gate, gain=gain, hp=True, name="gdn_gate_proj_s")
    o, conv_new, s_new = _gdn_step(proj, gates, conv_prev, s_prev, conv_w, alog_row, dtb_row, o_gain)
    y = _matmul(o.reshape(x.shape[0], -1), w_out, res=x, hp=True, name="gdn_out_proj_s")
    return y, conv_new, s_new


def _moba_layer_prompt(x, gain, w_in, w_out, *, batch, seq):
    proj = _matmul(x, w_in, gain=gain, tm=512, name="att_in_proj")
    cos_t, sin_a, sin_b = _rope_tables(jnp.arange(seq, dtype=jnp.int32))
    o, k_rot, v = _moba_prompt(proj, cos_t, sin_a, sin_b, batch=batch, seq=seq)
    y = _matmul(o, w_out, res=x, tm=512, name="att_out_proj")
    return y, k_rot, v


def _moba_layer_step(x, gain, cache_k, cache_v, page_table, w_in, w_out):
    b = x.shape[0]
    hd = HEAD_DIM
    n_pool, page = cache_k.shape[0], cache_k.shape[1]
    past_len = page_table.shape[1] * page
    own = past_len // MOBA_BLOCK
    assert own * MOBA_BLOCK == past_len and own >= 1
    n_sel = min(MOBA_TOPK, own)
    ppb = MOBA_BLOCK // page
    proj = _matmul(x, w_in, gain=gain, hp=True, tn=1024, name="att_in_proj_s")
    ck = cache_k.reshape(n_pool, page * KV_HEADS, hd)
    cv = cache_v.reshape(n_pool, page * KV_HEADS, hd)
    kmeans = _cache_block_means(ck, page_table)
    cos_t, sin_a, sin_b = _rope_tables(jnp.full((1,), past_len, jnp.int32))
    q, k_rot, sel = _moba_select(proj, kmeans, cos_t, sin_a, sin_b, n_sel=n_sel)
    logical = sel[:, :, :n_sel, None] * ppb + jnp.arange(ppb, dtype=jnp.int32)
    pages = jnp.take_along_axis(page_table, logical.reshape(b, -1), axis=1).reshape(-1)
    v_new = proj[:, (ATT_HEADS + KV_HEADS) * hd:]
    o = _moba_step(q, k_rot.reshape(b, KV_HEADS, hd), v_new.reshape(b, KV_HEADS, hd), ck, cv, pages)
    y = _matmul(o, w_out, res=x, hp=True, name="att_out_proj_s")
    return y, k_rot.reshape(b, KV_HEADS * hd), v_new


def _norm_kernel(x_ref, g_ref, o_ref):
    o_ref[...] = _rms(x_ref[...], g_ref[...])


def _final_norm(x, gain, *, tm=512):
    m, d = x.shape
    tm = min(tm, m)
    return pl.pallas_call(
        _norm_kernel,
        out_shape=jax.ShapeDtypeStruct((m, d), F32),
        grid=(m // tm,),
        in_specs=[pl.BlockSpec((tm, d), lambda i: (i, 0)), pl.BlockSpec((1, d), lambda i: (0, 0))],
        out_specs=pl.BlockSpec((tm, d), lambda i: (i, 0)),
        compiler_params=_params("parallel"),
        name="final_norm",
    )(x, gain.reshape(1, d))


def kernel(x_prompt, x_sample, state_conv, state_delta, cache_k, cache_v, page_table, norm_mix, norm_ffn,
           norm_final, gdn_w_in, gdn_conv_w, gdn_a_log, gdn_dt_bias, gdn_o_norm, gdn_w_out, att_w_in,
           att_w_out, ffn_w_gate, ffn_w_up, ffn_w_down, moe_router, moe_w_gate, moe_w_up, moe_w_down):
    bp, seq, d = x_prompt.shape
    db, dec_seq, _ = x_sample.shape
    assert dec_seq == 1
    depth = norm_mix.shape[0]
    hp = x_prompt.reshape(bp * seq, d)
    hs = x_sample.reshape(db, d)
    conv_p, delta_p, k_p, v_p = [], [], [], []
    conv_s, delta_s, k_s, v_s = [], [], [], []
    for i in range(depth):
        j = i // 2
        last = i == depth - 1
        if i % 2 == 0:
            gdn = (gdn_w_in[j], gdn_conv_w[j], gdn_a_log[j], gdn_dt_bias[j], gdn_o_norm[j], gdn_w_out[j])
            hp, cp, sp = _gdn_layer_prompt(hp, norm_mix[i], *gdn, batch=bp, seq=seq)
            hs, cs, ss = _gdn_layer_step(hs, norm_mix[i], state_conv[j], state_delta[j], *gdn)
            conv_p.append(cp)
            delta_p.append(sp.astype(state_delta.dtype))
            conv_s.append(cs)
            delta_s.append(ss.astype(state_delta.dtype))
            ffn = (norm_ffn[i], ffn_w_gate[j], ffn_w_up[j], ffn_w_down[j])
            hp = _ffn(hp, *ffn)
            hs = _ffn(hs, *ffn, hp=True)
            if last:
                hp = _final_norm(hp, norm_final)
                hs = _final_norm(hs, norm_final)
        else:
            hp, kp, vp = _moba_layer_prompt(hp, norm_mix[i], att_w_in[j], att_w_out[j], batch=bp, seq=seq)
            hs, kn, vn = _moba_layer_step(hs, norm_mix[i], cache_k[j], cache_v[j], page_table,
                                          att_w_in[j], att_w_out[j])
            k_p.append(kp.reshape(bp, seq, KV_HEADS, HEAD_DIM))
            v_p.append(vp.reshape(bp, seq, KV_HEADS, HEAD_DIM))
            k_s.append(kn.reshape(db, 1, KV_HEADS, HEAD_DIM))
            v_s.append(vn.reshape(db, 1, KV_HEADS, HEAD_DIM))
            moe = (norm_ffn[i], moe_router[j], moe_w_gate[j], moe_w_up[j], moe_w_down[j])
            fg = norm_final if last else None
            hp = _moe_layer(hp, *moe, fg, tm=512, tf=1792)
            hs = _moe_layer(hs, *moe, fg, tm=32, tf=1792 if last else 512, hp=not last)
    return (hp.reshape(bp, seq, d), hs.reshape(db, 1, d),
            jnp.stack(conv_p), jnp.stack(delta_p), jnp.stack(k_p), jnp.stack(v_p),
            jnp.stack(conv_s), jnp.stack(delta_s), jnp.stack(k_s), jnp.stack(v_s))
```

```python
import functools
import math

import jax
import jax.numpy as jnp
from jax import lax
from jax.experimental import pallas as pl
from jax.experimental.pallas import tpu as pltpu

F32 = jnp.float32
BF16 = jnp.bfloat16

GDN_HEADS = 8
GDN_DK = 128
GDN_DV = 128
CONV_W = 4
GDN_CHUNK = 64
ATT_HEADS = 8
KV_HEADS = 4
HEAD_DIM = 128
ROT_DIM = HEAD_DIM // 4
ROPE_THETA = 500000.0
MOBA_BLOCK = 256
MOBA_TOPK = 3
N_EXPERTS = 8
EPS = 1e-6

LANES = 128
SUBLANES = 8
VMEM_LIMIT = 56 * 1024 * 1024

NEG_INF = float("-inf")
_ISSUE_UNROLL = 8


def _params(*sem):
    return pltpu.CompilerParams(dimension_semantics=sem, vmem_limit_bytes=VMEM_LIMIT)


def _rms(x, gain):
    return x * lax.rsqrt(jnp.mean(x * x, axis=-1, keepdims=True) + EPS) * gain


def _silu(x):
    return x * jax.nn.sigmoid(x)


def _softplus(x):
    return jnp.maximum(x, 0.0) + jnp.log1p(jnp.exp(-jnp.abs(x)))


def _split2(a):
    hi = a.astype(BF16)
    lo = (a - hi.astype(F32)).astype(BF16)
    return hi, lo


def _split3(a):
    hi = a.astype(BF16)
    r = a - hi.astype(F32)
    mid = r.astype(BF16)
    lo = (r - mid.astype(F32)).astype(BF16)
    return hi, mid, lo


_NN = (((1,), (0,)), ((), ()))
_NT = (((1,), (1,)), ((), ()))
_TN = (((0,), (0,)), ((), ()))


def _dot(a, b, dims=_NN):
    return lax.dot_general(a, b, dims, preferred_element_type=F32)


def _dot_hp(a, b, dims=_NN):
    ah, al = _split2(a)
    bh, bl = _split2(b)
    return _dot(ah, bh, dims) + (_dot(al, bh, dims) + _dot(ah, bl, dims))


def _mm_kernel(*refs, has_gain, has_res, hp):
    it = iter(refs)
    a_ref = next(it)
    g_ref = next(it) if has_gain else None
    w_ref = next(it)
    r_ref = next(it) if has_res else None
    o_ref = next(it)
    an_ref = next(it)

    @pl.when(pl.program_id(1) == 0)
    def _():
        a = a_ref[...].astype(F32)
        if has_gain:
            a = _rms(a, g_ref[...])
        an_ref[...] = a.astype(an_ref.dtype)

    acc = (_dot_hp if hp else _dot)(an_ref[...], w_ref[...])
    if has_res:
        acc = acc + r_ref[...]
    o_ref[...] = acc.astype(o_ref.dtype)


def _matmul(a, w, *, gain=None, res=None, hp=False, tm=512, tn=None, out_dtype=F32, name="matmul"):
    m, k = a.shape
    n = w.shape[1]
    tm = min(tm, m)
    tn = n if tn is None else min(tn, n)
    assert m % tm == 0 and n % tn == 0
    args = [a]
    in_specs = [pl.BlockSpec((tm, k), lambda i, j: (i, 0))]
    if gain is not None:
        args.append(gain.reshape(1, k).astype(F32))
        in_specs.append(pl.BlockSpec((1, k), lambda i, j: (0, 0)))
    args.append(w if hp else w.astype(BF16))
    in_specs.append(pl.BlockSpec((k, tn), lambda i, j: (0, j)))
    if res is not None:
        args.append(res)
        in_specs.append(pl.BlockSpec((tm, tn), lambda i, j: (i, j)))
    scratch = [pltpu.VMEM((tm, k), F32 if hp else BF16)]
    return pl.pallas_call(
        functools.partial(_mm_kernel, has_gain=gain is not None, has_res=res is not None, hp=hp),
        out_shape=jax.ShapeDtypeStruct((m, n), out_dtype),
        grid=(m // tm, n // tn),
        in_specs=in_specs,
        out_specs=pl.BlockSpec((tm, tn), lambda i, j: (i, j)),
        scratch_shapes=scratch,
        compiler_params=_params("parallel", "arbitrary"),
        name=name,
    )(*args)


def _swiglu_step(xn, wg, wu, wd, hp):
    dot = _dot_hp if hp else _dot
    act = _silu(dot(xn, wg)) * dot(xn, wu)
    return dot(act if hp else act.astype(BF16), wd)


def _ffn_kernel(x_ref, g_ref, wg_ref, wu_ref, wd_ref, o_ref, xn_ref, acc_ref, *, hp):
    f = pl.program_id(1)

    @pl.when(f == 0)
    def _():
        xn_ref[...] = _rms(x_ref[...], g_ref[...]).astype(xn_ref.dtype)
        acc_ref[...] = jnp.zeros_like(acc_ref)

    acc_ref[...] += _swiglu_step(xn_ref[...], wg_ref[...], wu_ref[...], wd_ref[...], hp)

    @pl.when(f == pl.num_programs(1) - 1)
    def _():
        o_ref[...] = x_ref[...] + acc_ref[...]


def _ffn(x, gain, wg, wu, wd, *, hp=False, tm=1024, tf=512):
    m, d = x.shape
    dff = wg.shape[1]
    tm = min(tm, m)
    assert m % tm == 0 and dff % tf == 0
    wdt = F32 if hp else BF16
    return pl.pallas_call(
        functools.partial(_ffn_kernel, hp=hp),
        out_shape=jax.ShapeDtypeStruct((m, d), F32),
        grid=(m // tm, dff // tf),
        in_specs=[
            pl.BlockSpec((tm, d), lambda i, f: (i, 0)),
            pl.BlockSpec((1, d), lambda i, f: (0, 0)),
            pl.BlockSpec((d, tf), lambda i, f: (0, f)),
            pl.BlockSpec((d, tf), lambda i, f: (0, f)),
            pl.BlockSpec((tf, d), lambda i, f: (f, 0)),
        ],
        out_specs=pl.BlockSpec((tm, d), lambda i, f: (i, 0)),
        scratch_shapes=[pltpu.VMEM((tm, d), wdt), pltpu.VMEM((tm, d), F32)],
        compiler_params=_params("parallel", "arbitrary"),
        name="ffn",
    )(x, gain.reshape(1, d), wg.astype(wdt), wu.astype(wdt), wd.astype(wdt))


def _bdot(a, b, dims=_NN):
    return _dot(a.astype(BF16), b.astype(BF16), dims)


def _neumann_inverse_minus_eye(mats):
    c = mats[0].shape[0]
    ps = [-a for a in mats]
    ns = list(ps)
    for _ in range(int(math.log2(c)) - 1):
        ps = [_bdot(p, p) for p in ps]
        ns = [n + p + _bdot(n, p) for n, p in zip(ns, ps)]
    return ns


def _gdn_kernel(proj_ref, gate_ref, convw_ref, alog_ref, dtb_ref, ogain_ref,
                o_ref, sfin_ref, s_ref, tail_ref, *, tt, c):
    h_n, dk, dv = GDN_HEADS, GDN_DK, GDN_DV
    t = pl.program_id(1)

    @pl.when(t == 0)
    def _():
        s_ref[...] = jnp.zeros_like(s_ref)
        tail_ref[...] = jnp.zeros_like(tail_ref)

    gt = gate_ref[...]
    beta_all = jax.nn.sigmoid(gt)
    g_all = -jnp.exp(alog_ref[...]) * _softplus(gt + dtb_ref[...])

    ri = lax.broadcasted_iota(jnp.int32, (c, c), 0)
    ci = lax.broadcasted_iota(jnp.int32, (c, c), 1)
    tri = ri >= ci
    strict = ri > ci
    tri_b = tri.astype(BF16)
    ri2 = lax.broadcasted_iota(jnp.int32, (c, LANES), 0)
    ci2 = lax.broadcasted_iota(jnp.int32, (c, LANES), 1)
    upper_b = (ri2 <= ci2).astype(BF16)

    n_chunks = tt // c
    gams, gam_ts = [], []
    for ck in range(n_chunks):
        parts = _split3(g_all[ck * c:(ck + 1) * c, :])
        gam = sum(_dot(tri_b, p) for p in parts)
        gam_t = sum(_dot(p, upper_b, _TN) for p in parts)
        gams.append(gam)
        gam_ts.append(gam_t)

    def conv_silu(off):
        raw = proj_ref[:, off:off + LANES]
        xs = jnp.concatenate([tail_ref[:, off:off + LANES], raw], axis=0)
        w = convw_ref[:, off:off + LANES]
        acc = raw * w[CONV_W - 1:CONV_W]
        for j in range(CONV_W - 1):
            lo = SUBLANES - (CONV_W - 1) + j
            acc = acc + xs[lo:lo + tt] * w[j:j + 1]
        return _silu(acc)

    def l2n(x):
        return x * lax.rsqrt(jnp.sum(x * x, axis=-1, keepdims=True) + 1e-6)

    heads = range(h_n)
    pairs = [(ck, h) for ck in range(n_chunks) for h in heads]
    z_off = 2 * h_n * dk + h_n * dv
    q_all = [l2n(conv_silu(h * dk)) * (dk ** -0.5) for h in heads]
    k_all = [l2n(conv_silu(h_n * dk + h * dk)) for h in heads]
    v_all = [conv_silu(2 * h_n * dk + h * dv) for h in heads]

    def rows(x, ck):
        return x[ck * c:(ck + 1) * c]

    qc = [rows(q_all[h], ck) for ck, h in pairs]
    kc = [rows(k_all[h], ck) for ck, h in pairs]
    bcol = [rows(beta_all, ck)[:, h:h + 1] for ck, h in pairs]
    col = [gams[ck][:, h_n + h:h_n + h + 1] for ck, h in pairs]
    dm = [jnp.exp(jnp.where(tri, cl - gam_ts[ck][h_n + h:h_n + h + 1, :c], NEG_INF))
          for cl, (ck, h) in zip(col, pairs)]
    kb = [x * b for x, b in zip(kc, bcol)]
    kq = [_bdot(jnp.concatenate([x, y], axis=0), z, _NT) for x, y, z in zip(kb, qc, kc)]
    a = [jnp.where(strict, x[:c] * d, 0.0) for x, d in zip(kq, dm)]
    attn = [(x[c:] * d).astype(BF16) for x, d in zip(kq, dm)]
    ninv = _neumann_inverse_minus_eye(a)
    eg = [jnp.exp(cl) for cl in col]
    rhs = [jnp.concatenate([rows(v_all[h], ck) * b, x * e], axis=1)
           for (ck, h), b, x, e in zip(pairs, bcol, kb, eg)]
    sol = [r + _bdot(n, r) for n, r in zip(ninv, rhs)]
    glast = [cl[c - 1:c, :] for cl in col]
    lhs = [jnp.concatenate([x[:, dv:], y * e], axis=0).astype(BF16)
           for x, y, e in zip(sol, qc, eg)]
    k_dec = [(x * jnp.exp(g - cl)).astype(BF16) for x, g, cl in zip(kc, glast, col)]
    g_end = [jnp.exp(g) for g in glast]

    ogain = ogain_ref[...]
    for ck in range(n_chunks):
        idx = [ck * h_n + h for h in heads]
        s_old = [s_ref[h] for h in heads]
        both = [_dot(lhs[i], s.astype(BF16)) for i, s in zip(idx, s_old)]
        ub = [(sol[i][:, :dv] - b[:c]).astype(BF16) for i, b in zip(idx, both)]
        o = [b[c:] + _dot(attn[i], u) for i, b, u in zip(idx, both, ub)]
        for h, i, s, u in zip(heads, idx, s_old, ub):
            s_ref[h] = s * g_end[i] + _dot(k_dec[i], u, _TN)
        for h, x in zip(heads, o):
            zc = proj_ref[ck * c:(ck + 1) * c, z_off + h * dv:z_off + (h + 1) * dv]
            o_ref[ck * c:(ck + 1) * c, h * dv:(h + 1) * dv] = (_rms(x, ogain) * _silu(zc)).astype(o_ref.dtype)

    tail_ref[...] = proj_ref[tt - SUBLANES:tt, :tail_ref.shape[1]]

    @pl.when(t == pl.num_programs(1) - 1)
    def _():
        sfin_ref[0] = s_ref[...]


def _gdn_prompt(proj, gates, conv_w, alog_row, dtb_row, o_gain, *, batch, seq, tt=256):
    h_n, dk, dv = GDN_HEADS, GDN_DK, GDN_DV
    conv_dim = h_n * (2 * dk + dv)
    width = proj.shape[1]
    tt = min(tt, seq)
    c = math.gcd(seq, GDN_CHUNK)
    assert seq % tt == 0 and tt % c == 0 and tt >= SUBLANES
    nt = seq // tt
    return pl.pallas_call(
        functools.partial(_gdn_kernel, tt=tt, c=c),
        out_shape=(jax.ShapeDtypeStruct((batch * seq, h_n * dv), BF16),
                   jax.ShapeDtypeStruct((batch, h_n, dk, dv), F32)),
        grid=(batch, nt),
        in_specs=[
            pl.BlockSpec((tt, width), lambda b, t: (b * nt + t, 0)),
            pl.BlockSpec((tt, LANES), lambda b, t: (b * nt + t, 0)),
            pl.BlockSpec((CONV_W, conv_dim), lambda b, t: (0, 0)),
            pl.BlockSpec((1, LANES), lambda b, t: (0, 0)),
            pl.BlockSpec((1, LANES), lambda b, t: (0, 0)),
            pl.BlockSpec((1, dv), lambda b, t: (0, 0)),
        ],
        out_specs=(pl.BlockSpec((tt, h_n * dv), lambda b, t: (b * nt + t, 0)),
                   pl.BlockSpec((1, h_n, dk, dv), lambda b, t: (b, 0, 0, 0))),
        scratch_shapes=[pltpu.VMEM((h_n, dk, dv), F32), pltpu.VMEM((SUBLANES, conv_dim), F32)],
        compiler_params=_params("parallel", "arbitrary"),
        name="gdn_prompt",
    )(proj, gates, conv_w, alog_row, dtb_row, o_gain.reshape(1, dv))


def _gdn_step_kernel(proj_ref, gate_ref, cprev_ref, s_ref, convw_ref, alog_ref, dtb_ref, ogain_ref,
                     o_ref, cnew_ref, snew_ref):
    h_n, dk, dv = GDN_HEADS, GDN_DK, GDN_DV
    conv_dim = h_n * (2 * dk + dv)
    u_row = proj_ref[0]
    prev = cprev_ref[0]
    w = convw_ref[...]
    raw = u_row[:, :conv_dim]
    acc = raw * w[CONV_W - 1:CONV_W]
    for j in range(CONV_W - 1):
        acc = acc + prev[j:j + 1] * w[j:j + 1]
    qkv = _silu(acc)
    for j in range(CONV_W - 2):
        cnew_ref[0, j:j + 1, :] = prev[j + 1:j + 2]
    cnew_ref[0, CONV_W - 2:CONV_W - 1, :] = raw

    gt = gate_ref[0]
    beta_all = jax.nn.sigmoid(gt)
    g_all = -jnp.exp(alog_ref[...]) * _softplus(gt + dtb_ref[...])
    rows = lax.broadcasted_iota(jnp.int32, (SUBLANES, LANES), 0)
    ogain = ogain_ref[...]

    def l2n(x):
        return x * lax.rsqrt(jnp.sum(x * x, axis=-1, keepdims=True) + 1e-6)

    def as_row0(x, y=None):
        out = jnp.where(rows == 0, jnp.broadcast_to(x, (SUBLANES, LANES)), 0.0)
        if y is not None:
            out = jnp.where(rows == 1, jnp.broadcast_to(y, (SUBLANES, LANES)), out)
        return out

    for h in range(h_n):
        q = l2n(qkv[:, h * dk:(h + 1) * dk]) * (dk ** -0.5)
        k = l2n(qkv[:, h_n * dk + h * dk:h_n * dk + (h + 1) * dk])
        v = qkv[:, 2 * h_n * dk + h * dv:2 * h_n * dk + (h + 1) * dv]
        z = u_row[:, conv_dim + h * dv:conv_dim + (h + 1) * dv]
        beta = beta_all[:, h:h + 1]
        eg = jnp.exp(g_all[:, h_n + h:h_n + h + 1])
        s = s_ref[0, h]
        kcd = k * (beta * eg)
        q_dec = q * eg
        both = _dot_hp(as_row0(kcd, q_dec), s)
        u = v * beta - both[0:1]
        o = both[1:2] + jnp.sum(q * k, axis=-1, keepdims=True) * u
        snew_ref[0, h] = s * eg + _dot_hp(as_row0(k), as_row0(u), _TN)
        o_ref[0, :, h * dv:(h + 1) * dv] = (_rms(o, ogain) * _silu(z)).astype(o_ref.dtype)


def _gdn_step(proj, gates, conv_prev, s_prev, conv_w, alog_row, dtb_row, o_gain):
    h_n, dk, dv = GDN_HEADS, GDN_DK, GDN_DV
    conv_dim = h_n * (2 * dk + dv)
    b, width = proj.shape
    return pl.pallas_call(
        _gdn_step_kernel,
        out_shape=(jax.ShapeDtypeStruct((b, 1, h_n * dv), F32),
                   jax.ShapeDtypeStruct((b, CONV_W - 1, conv_dim), F32),
                   jax.ShapeDtypeStruct((b, h_n, dk, dv), F32)),
        grid=(b,),
        in_specs=[
            pl.BlockSpec((1, 1, width), lambda i: (i, 0, 0)),
            pl.BlockSpec((1, 1, LANES), lambda i: (i, 0, 0)),
            pl.BlockSpec((1, CONV_W - 1, conv_dim), lambda i: (i, 0, 0)),
            pl.BlockSpec((1, h_n, dk, dv), lambda i: (i, 0, 0, 0)),
            pl.BlockSpec((CONV_W, conv_dim), lambda i: (0, 0)),
            pl.BlockSpec((1, LANES), lambda i: (0, 0)),
            pl.BlockSpec((1, LANES), lambda i: (0, 0)),
            pl.BlockSpec((1, dv), lambda i: (0, 0)),
        ],
        out_specs=(pl.BlockSpec((1, 1, h_n * dv), lambda i: (i, 0, 0)),
                   pl.BlockSpec((1, CONV_W - 1, conv_dim), lambda i: (i, 0, 0)),
                   pl.BlockSpec((1, h_n, dk, dv), lambda i: (i, 0, 0, 0))),
        compiler_params=_params("parallel"),
        name="gdn_step",
    )(proj.reshape(b, 1, width), gates.reshape(b, 1, LANES), conv_prev, s_prev, conv_w,
      alog_row, dtb_row, o_gain.reshape(1, dv))


def _rope(x, cos, sin_a, sin_b):
    half = ROT_DIM // 2
    return (x * cos + pltpu.roll(x, HEAD_DIM - half, 1) * sin_a + pltpu.roll(x, half, 1) * sin_b)


def _rope_tables(pos):
    half = ROT_DIM // 2
    inv = jnp.power(ROPE_THETA, -jnp.arange(half, dtype=F32) * (2.0 / ROT_DIM))
    ang = pos.astype(F32)[:, None] * inv[None, :]
    cos, sin = jnp.cos(ang), jnp.sin(ang)
    n = pos.shape[0]
    rest = HEAD_DIM - ROT_DIM
    cos_t = jnp.concatenate([cos, cos, jnp.ones((n, rest), F32)], axis=1)
    sin_a = jnp.concatenate([-sin, jnp.zeros((n, half + rest), F32)], axis=1)
    sin_b = jnp.concatenate([jnp.zeros((n, half), F32), sin, jnp.zeros((n, rest), F32)], axis=1)
    return cos_t, sin_a, sin_b


def _moba_kernel(q_ref, k_ref, v_ref, cos_ref, sa_ref, sb_ref, o_ref, kf_ref, vf_ref,
                 kout_ref, kb_ref, vb_ref, *, seq):
    blk, hd = MOBA_BLOCK, HEAD_DIM
    group = ATT_HEADS // KV_HEADS
    nb = seq // blk
    nq = group * blk
    scale = hd ** -0.5

    kout_ref[...] = _rope(k_ref[...], cos_ref[...], sa_ref[...], sb_ref[...])
    kb_ref[...] = kout_ref[...].astype(BF16)
    vb_ref[...] = v_ref[...].astype(BF16)
    for gg in range(KV_HEADS):
        @pl.when(pl.program_id(1) == gg)
        def _(gg=gg):
            kf_ref[pl.ds(gg, seq, stride=KV_HEADS), :] = kout_ref[...]
            vf_ref[pl.ds(gg, seq, stride=KV_HEADS), :] = v_ref[...]

    kmean = jnp.concatenate(
        [jnp.sum(kout_ref[n * blk:(n + 1) * blk, :], axis=0, keepdims=True) for n in range(nb)],
        axis=0) * (1.0 / blk)

    blk_id = lax.broadcasted_iota(jnp.int32, (nb, nq), 0)
    key_r = lax.broadcasted_iota(jnp.int32, (blk, nq), 0)
    qry_c = lax.broadcasted_iota(jnp.int32, (blk, nq), 1) % blk
    causal = key_r <= qry_c

    for i in range(nb):
        rows = slice(i * blk, (i + 1) * blk)
        cos, sa, sb = cos_ref[rows, :], sa_ref[rows, :], sb_ref[rows, :]
        qs = [_rope(q_ref[rows, g * hd:(g + 1) * hd], cos, sa, sb) for g in range(group)]
        q = jnp.concatenate(qs, axis=0)
        qb = (q * (scale * math.log2(math.e))).astype(BF16)

        if i > MOBA_TOPK:
            gate = jnp.where(blk_id < i, _dot_hp(kmean, q, _NT), NEG_INF)
            rank = jnp.zeros((nb, nq), F32)
            for m in range(i):
                gm = gate[m:m + 1, :]
                rank = rank + jnp.where((gm > gate) | ((gm == gate) & (m < blk_id)), 1.0, 0.0)
            sel = jnp.where((blk_id < i) & (rank < MOBA_TOPK), 1.0, 0.0)
        else:
            sel = None

        ss = []
        for j in range(i + 1):
            s = _dot(kb_ref[j * blk:(j + 1) * blk, :], qb, _NT)
            if j == i:
                s = jnp.where(causal, s, NEG_INF)
            elif sel is not None:
                s = jnp.where(sel[j:j + 1, :] > 0.0, s, NEG_INF)
            ss.append(s)
        m_all = functools.reduce(jnp.maximum, [jnp.max(s, axis=0, keepdims=True) for s in ss])
        ps = [jnp.exp2(s - m_all) for s in ss]
        l_all = functools.reduce(jnp.add, [jnp.sum(p, axis=0, keepdims=True) for p in ps])
        acc = functools.reduce(
            jnp.add, [_dot(vb_ref[j * blk:(j + 1) * blk, :], p.astype(BF16), _TN) for j, p in enumerate(ps)])
        out = (acc / l_all).T
        for g in range(group):
            o_ref[rows, g * hd:(g + 1) * hd] = out[g * blk:(g + 1) * blk].astype(o_ref.dtype)


def _moba_prompt(proj, cos_t, sin_a, sin_b, *, batch, seq):
    hd = HEAD_DIM
    group = ATT_HEADS // KV_HEADS
    assert seq % MOBA_BLOCK == 0
    tbl = pl.BlockSpec((seq, hd), lambda b, g: (0, 0))
    kv_rows = jax.ShapeDtypeStruct((batch * seq * KV_HEADS, hd), F32)
    kv_spec = pl.BlockSpec((seq * KV_HEADS, hd), lambda b, g: (b, 0))
    return pl.pallas_call(
        functools.partial(_moba_kernel, seq=seq),
        out_shape=(jax.ShapeDtypeStruct((batch * seq, ATT_HEADS * hd), BF16), kv_rows, kv_rows),
        grid=(batch, KV_HEADS),
        in_specs=[
            pl.BlockSpec((seq, group * hd), lambda b, g: (b, g)),
            pl.BlockSpec((seq, hd), lambda b, g: (b, ATT_HEADS + g)),
            pl.BlockSpec((seq, hd), lambda b, g: (b, ATT_HEADS + KV_HEADS + g)),
            tbl, tbl, tbl,
        ],
        out_specs=(pl.BlockSpec((seq, group * hd), lambda b, g: (b, g)), kv_spec, kv_spec),
        scratch_shapes=[pltpu.VMEM((seq, hd), F32), pltpu.VMEM((seq, hd), BF16), pltpu.VMEM((seq, hd), BF16)],
        compiler_params=_params("parallel", "arbitrary"),
        name="moba_prompt",
    )(proj, proj, proj, cos_t, sin_a, sin_b)


def _kmean_kernel(pt_ref, *refs, pages, pages_per_block):
    page_refs, o_ref = refs[:pages], refs[pages]
    rows = page_refs[0].shape[1]
    tokens = pages_per_block * rows // KV_HEADS
    for n in range(pages // pages_per_block):
        part = jnp.zeros((SUBLANES, HEAD_DIM), F32)
        for p in range(pages_per_block):
            x = page_refs[n * pages_per_block + p][0]
            part = part + jnp.sum(x.reshape(rows // SUBLANES, SUBLANES, HEAD_DIM), axis=0)
        acc = part[:KV_HEADS]
        for r in range(1, SUBLANES // KV_HEADS):
            acc = acc + part[r * KV_HEADS:(r + 1) * KV_HEADS]
        o_ref[0, n] = acc * (1.0 / tokens)


def _cache_block_means(cache, page_table, *, blocks_per_step=8):
    n_pool, rows, hd = cache.shape
    page = rows // KV_HEADS
    b, n_pages = page_table.shape
    ppb = MOBA_BLOCK // page
    assert ppb * page == MOBA_BLOCK and n_pages % ppb == 0 and rows % SUBLANES == 0
    n_blocks = n_pages // ppb
    bps = math.gcd(blocks_per_step, n_blocks)
    pages = bps * ppb
    specs = [pl.BlockSpec((1, rows, hd), lambda i, s, pt, p=p: (pt[i, s * pages + p], 0, 0))
             for p in range(pages)]
    return pl.pallas_call(
        functools.partial(_kmean_kernel, pages=pages, pages_per_block=ppb),
        out_shape=jax.ShapeDtypeStruct((b, n_blocks, KV_HEADS, hd), F32),
        grid_spec=pltpu.PrefetchScalarGridSpec(
            num_scalar_prefetch=1,
            grid=(b, n_blocks // bps),
            in_specs=specs,
            out_specs=pl.BlockSpec((1, bps, KV_HEADS, hd), lambda i, s, pt: (i, s, 0, 0)),
        ),
        compiler_params=_params("parallel", "arbitrary"),
        name="cache_block_means",
    )(page_table, *([cache] * pages))


def _moba_select_kernel(proj_ref, km_ref, cos_ref, sa_ref, sb_ref, q_ref, k_ref, sel_ref, *, n_sel):
    hd = HEAD_DIM
    group = ATT_HEADS // KV_HEADS
    row = proj_ref[0]
    cos, sa, sb = cos_ref[...], sa_ref[...], sb_ref[...]
    rows = lax.broadcasted_iota(jnp.int32, (SUBLANES, hd), 0)
    q = jnp.zeros((SUBLANES, hd), F32)
    for h in range(ATT_HEADS):
        qh = _rope(row[:, h * hd:(h + 1) * hd], cos, sa, sb)
        q = jnp.where(rows == h, jnp.broadcast_to(qh, (SUBLANES, hd)), q)
    q_ref[0] = q
    for g in range(KV_HEADS):
        off = (ATT_HEADS + g) * hd
        k_ref[0, :, g * hd:(g + 1) * hd] = _rope(row[:, off:off + hd], cos, sa, sb)

    nb = km_ref.shape[1]
    hrow = lax.broadcasted_iota(jnp.int32, (ATT_HEADS, nb), 0)
    lane = lax.broadcasted_iota(jnp.int32, (ATT_HEADS, nb), 1)
    gate = jnp.zeros((ATT_HEADS, nb), F32)
    for g in range(KV_HEADS):
        gg = _dot_hp(q, km_ref[0, :, g, :], _NT)
        gate = jnp.where(hrow // group == g, gg, gate)
    out_lane = lax.broadcasted_iota(jnp.int32, (ATT_HEADS, LANES), 1)
    sel = jnp.zeros((ATT_HEADS, LANES), jnp.int32)
    for t in range(n_sel):
        m = jnp.max(gate, axis=-1, keepdims=True)
        idx = jnp.min(jnp.where(gate == m, lane, nb), axis=-1, keepdims=True)
        sel = jnp.where(out_lane == t, idx, sel)
        gate = jnp.where(lane == idx, NEG_INF, gate)
    sel_ref[0] = sel


def _moba_select(proj, kmeans, cos_t, sin_a, sin_b, *, n_sel):
    b, width = proj.shape
    nb = kmeans.shape[1]
    hd = HEAD_DIM
    tbl = pl.BlockSpec((1, hd), lambda i: (0, 0))
    return pl.pallas_call(
        functools.partial(_moba_select_kernel, n_sel=n_sel),
        out_shape=(jax.ShapeDtypeStruct((b, ATT_HEADS, hd), F32),
                   jax.ShapeDtypeStruct((b, 1, KV_HEADS * hd), F32),
                   jax.ShapeDtypeStruct((b, ATT_HEADS, LANES), jnp.int32)),
        grid=(b,),
        in_specs=[pl.BlockSpec((1, 1, width), lambda i: (i, 0, 0)),
                  pl.BlockSpec((1, nb, KV_HEADS, hd), lambda i: (i, 0, 0, 0)),
                  tbl, tbl, tbl],
        out_specs=(pl.BlockSpec((1, ATT_HEADS, hd), lambda i: (i, 0, 0)),
                   pl.BlockSpec((1, 1, KV_HEADS * hd), lambda i: (i, 0, 0)),
                   pl.BlockSpec((1, ATT_HEADS, LANES), lambda i: (i, 0, 0))),
        compiler_params=_params("parallel"),
        name="moba_select",
    )(proj.reshape(b, 1, width), kmeans, cos_t, sin_a, sin_b)


def _moba_step_kernel(pg_ref, q_ref, kn_ref, vn_ref, *refs, n_pages, heads):
    n = heads * n_pages
    k_refs, v_refs, o_ref = refs[:n], refs[n:2 * n], refs[2 * n]
    hd = HEAD_DIM
    group = ATT_HEADS // KV_HEADS
    scale = hd ** -0.5
    page = k_refs[0].shape[1] // KV_HEADS

    for hh in range(heads):
        h = pl.program_id(1) * heads + hh
        g = h // group

        def head_rows(ref, g=g):
            out = ref[0, pl.ds(0, page, stride=KV_HEADS), :]
            for gg in range(1, KV_HEADS):
                out = jnp.where(g == gg, ref[0, pl.ds(gg, page, stride=KV_HEADS), :], out)
            return out

        q = q_ref[0, pl.ds(h, 1), :]
        q8 = jnp.broadcast_to(q, (SUBLANES, hd))
        kn = kn_ref[0, pl.ds(g, 1), :]
        vn = vn_ref[0, pl.ds(g, 1), :]
        s_self = jnp.sum(q * kn, axis=-1, keepdims=True) * scale
        scores = [_dot_hp(q8, head_rows(k_refs[hh * n_pages + p]), _NT)[0:1] * scale
                  for p in range(n_pages)]
        m = functools.reduce(jnp.maximum, [jnp.max(s, axis=-1, keepdims=True) for s in scores], s_self)
        p_self = jnp.exp(s_self - m)
        l = p_self
        acc = p_self * vn
        for p in range(n_pages):
            e = jnp.exp(scores[p] - m)
            l = l + jnp.sum(e, axis=-1, keepdims=True)
            acc = acc + _dot_hp(jnp.broadcast_to(e, (SUBLANES, page)), head_rows(v_refs[hh * n_pages + p]))[0:1]
        o_ref[0, hh:hh + 1, :] = (acc / l).astype(o_ref.dtype)


def _moba_step(q, k_new, v_new, cache_k, cache_v, pages, *, heads=4):
    b = q.shape[0]
    hd = HEAD_DIM
    rows = cache_k.shape[1]
    n_pages = pages.shape[0] // (b * ATT_HEADS)
    assert ATT_HEADS % heads == 0
    steps = ATT_HEADS // heads

    def page_spec(k):
        return pl.BlockSpec((1, rows, hd),
                            lambda i, s, pg, k=k: (pg[(i * ATT_HEADS + s * heads) * n_pages + k], 0, 0))

    specs = ([pl.BlockSpec((1, ATT_HEADS, hd), lambda i, s, pg: (i, 0, 0)),
              pl.BlockSpec((1, KV_HEADS, hd), lambda i, s, pg: (i, 0, 0)),
              pl.BlockSpec((1, KV_HEADS, hd), lambda i, s, pg: (i, 0, 0))]
             + [page_spec(k) for k in range(heads * n_pages)] * 2)
    out = pl.pallas_call(
        functools.partial(_moba_step_kernel, n_pages=n_pages, heads=heads),
        out_shape=jax.ShapeDtypeStruct((b * steps, heads, hd), F32),
        grid_spec=pltpu.PrefetchScalarGridSpec(
            num_scalar_prefetch=1,
            grid=(b, steps),
            in_specs=specs,
            out_specs=pl.BlockSpec((1, heads, hd), lambda i, s, pg: (i * steps + s, 0, 0)),
        ),
        compiler_params=_params("parallel", "arbitrary"),
        name="moba_step",
    )(pages, q, k_new, v_new, *([cache_k] * (heads * n_pages)), *([cache_v] * (heads * n_pages)))
    return out.reshape(b, ATT_HEADS * hd)


def _router_kernel(x_ref, g_ref, w_ref, idx_ref, wts_ref):
    logits = _dot_hp(_rms(x_ref[...], g_ref[...]), w_ref[...])
    lane = lax.broadcasted_iota(jnp.int32, logits.shape, 1)
    logits = jnp.where(lane < N_EXPERTS, logits, NEG_INF)
    m1 = jnp.max(logits, axis=-1, keepdims=True)
    i1 = jnp.min(jnp.where(logits == m1, lane, LANES), axis=-1, keepdims=True)
    rest = jnp.where(lane == i1, NEG_INF, logits)
    m2 = jnp.max(rest, axis=-1, keepdims=True)
    i2 = jnp.min(jnp.where(rest == m2, lane, LANES), axis=-1, keepdims=True)
    e = jnp.exp(m2 - m1)
    w1 = 1.0 / (1.0 + e)
    idx_ref[...] = jnp.where(lane == 0, i1, jnp.where(lane == 1, i2, 0))
    wts_ref[...] = jnp.where(lane == 0, w1, jnp.where(lane == 1, e * w1, 0.0))


def _router(x, gain, w_router, *, tm=512):
    m, d = x.shape
    tm = min(tm, m)
    assert m % tm == 0
    w = jnp.pad(w_router, ((0, 0), (0, LANES - w_router.shape[1])))
    return pl.pallas_call(
        _router_kernel,
        out_shape=(jax.ShapeDtypeStruct((m, LANES), jnp.int32), jax.ShapeDtypeStruct((m, LANES), F32)),
        grid=(m // tm,),
        in_specs=[pl.BlockSpec((tm, d), lambda i: (i, 0)), pl.BlockSpec((1, d), lambda i: (0, 0)),
                  pl.BlockSpec((d, LANES), lambda i: (0, 0))],
        out_specs=(pl.BlockSpec((tm, LANES), lambda i: (i, 0)), pl.BlockSpec((tm, LANES), lambda i: (i, 0))),
        compiler_params=_params("parallel"),
        name="router",
    )(x, gain.reshape(1, d), w)


def _row_copy(src, dst, sem, i, j):
    return pltpu.make_async_copy(src.at[pl.ds(i, 1)], dst.at[pl.ds(j, 1)], sem)


def _moe_ffn_kernel(te_ref, na_ref, src_ref, x_ref, g_ref, wg_ref, wu_ref, wd_ref, o_ref,
                    xbuf_ref, xn_ref, acc_ref, sem, *, tm, hp):
    t, f = pl.program_id(0), pl.program_id(1)
    last = f == pl.num_programs(1) - 1
    n_active = na_ref[0]
    active = t < n_active
    slot = t % 2

    def start_gather(tile, dst_slot):
        def issue(r, carry):
            _row_copy(x_ref, xbuf_ref.at[dst_slot], sem.at[dst_slot], src_ref[tile * tm + r], r).start()
            return carry

        lax.fori_loop(0, tm, issue, 0, unroll=_ISSUE_UNROLL)

    @pl.when(active)
    def _():
        @pl.when(f == 0)
        def _():
            @pl.when(t == 0)
            def _():
                start_gather(0, 0)

            pltpu.make_async_copy(x_ref.at[pl.ds(0, tm)], xbuf_ref.at[slot], sem.at[slot]).wait()
            xn_ref[...] = _rms(xbuf_ref[slot], g_ref[...]).astype(xn_ref.dtype)
            acc_ref[...] = jnp.zeros_like(acc_ref)

            @pl.when(t + 1 < n_active)
            def _():
                start_gather(t + 1, 1 - slot)

        acc_ref[...] += _swiglu_step(xn_ref[...], wg_ref[0], wu_ref[0], wd_ref[0], hp)

        @pl.when(last)
        def _():
            o_ref[...] = acc_ref[...]

    @pl.when(jnp.logical_and(jnp.logical_not(active), last))
    def _():
        o_ref[...] = jnp.zeros_like(o_ref)


def _moe_ffn(x, src_token, gain, wg, wu, wd, tile_expert, n_active, *, tm, tf, hp=False):
    d = x.shape[1]
    s = src_token.shape[0]
    dff = wg.shape[2]
    assert s % tm == 0 and dff % tf == 0
    wdt = F32 if hp else BF16
    return pl.pallas_call(
        functools.partial(_moe_ffn_kernel, tm=tm, hp=hp),
        out_shape=jax.ShapeDtypeStruct((s, d), F32),
        grid_spec=pltpu.PrefetchScalarGridSpec(
            num_scalar_prefetch=3,
            grid=(s // tm, dff // tf),
            in_specs=[
                pl.BlockSpec(memory_space=pl.ANY),
                pl.BlockSpec((1, d), lambda t, f, te, na, src: (0, 0)),
                pl.BlockSpec((1, d, tf), lambda t, f, te, na, src: (te[t], 0, f)),
                pl.BlockSpec((1, d, tf), lambda t, f, te, na, src: (te[t], 0, f)),
                pl.BlockSpec((1, tf, d), lambda t, f, te, na, src: (te[t], f, 0)),
            ],
            out_specs=pl.BlockSpec((tm, d), lambda t, f, te, na, src: (t, 0)),
            scratch_shapes=[pltpu.VMEM((2, tm, d), F32), pltpu.VMEM((tm, d), wdt), pltpu.VMEM((tm, d), F32),
                            pltpu.SemaphoreType.DMA((2,))],
        ),
        compiler_params=_params("arbitrary", "arbitrary"),
        name="moe_ffn",
    )(tile_expert, n_active, src_token, x, gain.reshape(1, d), wg.astype(wdt), wu.astype(wdt), wd.astype(wdt))


def _combine_kernel(slot_ref, x_ref, wts_ref, *refs, tc, has_norm):
    if has_norm:
        gain_ref, y_ref, o_ref, buf_ref, sem = refs
    else:
        y_ref, o_ref, buf_ref, sem = refs
    i = pl.program_id(0)
    slot = i % 2

    def start_gather(step, dst_slot):
        def issue(r, carry):
            for k in range(2):
                _row_copy(y_ref, buf_ref.at[dst_slot, k], sem.at[dst_slot],
                          slot_ref[2 * (step * tc + r) + k], r).start(priority=k)
            return carry

        lax.fori_loop(0, tc, issue, 0, unroll=_ISSUE_UNROLL)

    @pl.when(i == 0)
    def _():
        start_gather(0, 0)

    @pl.when(i + 1 < pl.num_programs(0))
    def _():
        start_gather(i + 1, 1 - slot)

    for k in range(2):
        pltpu.make_async_copy(y_ref.at[pl.ds(0, tc)], buf_ref.at[slot, k], sem.at[slot]).wait()
    w = wts_ref[...]
    out = x_ref[...] + (w[:, 0:1] * buf_ref[slot, 0] + w[:, 1:2] * buf_ref[slot, 1])
    if has_norm:
        out = _rms(out, gain_ref[...])
    o_ref[...] = out


def _combine(x, wts, y_sorted, slots, final_gain, *, tc=256):
    m, d = x.shape
    tc = min(tc, m)
    assert m % tc == 0
    has_norm = final_gain is not None
    args = [slots, x, wts]
    in_specs = [pl.BlockSpec((tc, d), lambda i, sl: (i, 0)), pl.BlockSpec((tc, LANES), lambda i, sl: (i, 0))]
    if has_norm:
        args.append(final_gain.reshape(1, d))
        in_specs.append(pl.BlockSpec((1, d), lambda i, sl: (0, 0)))
    args.append(y_sorted)
    in_specs.append(pl.BlockSpec(memory_space=pl.ANY))
    return pl.pallas_call(
        functools.partial(_combine_kernel, tc=tc, has_norm=has_norm),
        out_shape=jax.ShapeDtypeStruct((m, d), F32),
        grid_spec=pltpu.PrefetchScalarGridSpec(
            num_scalar_prefetch=1,
            grid=(m // tc,),
            in_specs=in_specs,
            out_specs=pl.BlockSpec((tc, d), lambda i, sl: (i, 0)),
            scratch_shapes=[pltpu.VMEM((2, 2, tc, d), F32), pltpu.SemaphoreType.DMA((2,))],
        ),
        compiler_params=_params("arbitrary"),
        name="moe_combine",
    )(*args)


def _invert_slots_kernel(slot_ref, src_ref, *, n_pairs, n_slots):
    def clear(s, carry):
        src_ref[s] = 0
        return carry

    def put(tok, carry):
        src_ref[slot_ref[2 * tok]] = tok
        src_ref[slot_ref[2 * tok + 1]] = tok
        return carry

    lax.fori_loop(0, n_slots, clear, 0, unroll=_ISSUE_UNROLL)
    lax.fori_loop(0, n_pairs // 2, put, 0, unroll=_ISSUE_UNROLL)


def _invert_slots(slot, n_slots):
    n_pairs = slot.shape[0]
    return pl.pallas_call(
        functools.partial(_invert_slots_kernel, n_pairs=n_pairs, n_slots=n_slots),
        out_shape=jax.ShapeDtypeStruct((n_slots,), jnp.int32),
        grid_spec=pltpu.PrefetchScalarGridSpec(
            num_scalar_prefetch=1,
            grid=(1,),
            in_specs=[],
            out_specs=pl.BlockSpec(memory_space=pltpu.SMEM),
        ),
        compiler_params=_params("arbitrary"),
        name="invert_slots",
    )(slot)


def _moe_layer(x, gain, w_router, wg, wu, wd, final_gain, *, tm, tf, hp=False):
    m, d = x.shape
    n_e = wg.shape[0]
    idx, wts = _router(x, gain, w_router)
    e_flat = idx[:, :2].reshape(-1)
    onehot = (e_flat[:, None] == jnp.arange(n_e, dtype=jnp.int32)[None, :]).astype(jnp.int32)
    csum = jnp.cumsum(onehot, axis=0)
    rank = jnp.sum(csum * onehot, axis=1) - 1
    tiles_e = (csum[-1] + tm - 1) // tm
    tile_end = jnp.cumsum(tiles_e)
    slot = ((tile_end - tiles_e)[e_flat] * tm + rank).astype(jnp.int32)
    n_tiles = (2 * m) // tm + n_e
    tile_expert = jnp.minimum(
        jnp.sum(jnp.arange(n_tiles, dtype=jnp.int32)[:, None] >= tile_end[None, :], axis=1), n_e - 1).astype(jnp.int32)
    n_active = tile_end[-1:].astype(jnp.int32)
    src_token = _invert_slots(slot, n_tiles * tm)
    ys = _moe_ffn(x, src_token, gain, wg, wu, wd, tile_expert, n_active, tm=tm, tf=tf, hp=hp)
    return _combine(x, wts, ys, slot, final_gain)


def _gdn_weights(w_in, a_log, dt_bias):
    h_n = GDN_HEADS
    main = 2 * h_n * GDN_DK + 2 * h_n * GDN_DV
    w_gate = jnp.pad(w_in[:, main:], ((0, 0), (0, LANES - 2 * h_n)))
    pad = (h_n, LANES - 2 * h_n)
    alog_row = jnp.pad(a_log.astype(F32), pad).reshape(1, LANES)
    dtb_row = jnp.pad(dt_bias.astype(F32), pad).reshape(1, LANES)
    return w_in[:, :main], w_gate, alog_row, dtb_row


def _gdn_layer_prompt(x, gain, w_in, conv_w, a_log, dt_bias, o_gain, w_out, *, batch, seq):
    conv_dim = GDN_HEADS * (2 * GDN_DK + GDN_DV)
    w_main, w_gate, alog_row, dtb_row = _gdn_weights(w_in, a_log, dt_bias)
    proj = _matmul(x, w_main, gain=gain, tm=512, name="gdn_in_proj")
    gates = _matmul(x, w_gate, gain=gain, hp=True, tm=512, name="gdn_gate_proj")
    o, s_fin = _gdn_prompt(proj, gates, conv_w, alog_row, dtb_row, o_gain, batch=batch, seq=seq)
    y = _matmul(o, w_out, res=x, tm=512, name="gdn_out_proj")
    conv_state = proj.reshape(batch, seq, -1)[:, seq - (CONV_W - 1):, :conv_dim]
    return y, conv_state, s_fin


def _gdn_layer_step(x, gain, conv_prev, s_prev, w_in, conv_w, a_log, dt_bias, o_gain, w_out):
    w_main, w_gate, alog_row, dtb_row = _gdn_weights(w_in, a_log, dt_bias)
    proj = _matmul(x, w_main, gain=gain, hp=True, tn=1024, name="gdn_in_proj_s")
    gates = _matmul(x, w_gate, gain=gain, hp=True, name="gdn_gate_proj_s")
    o, conv_new, s_new = _gdn_step(proj, gates, conv_prev, s_prev, conv_w, alog_row, dtb_row, o_gain)
    y = _matmul(o.reshape(x.shape[0], -1), w_out, res=x, hp=True, name="gdn_out_proj_s")
    return y, conv_new, s_new


def _moba_layer_prompt(x, gain, w_in, w_out, *, batch, seq):
    proj = _matmul(x, w_in, gain=gain, tm=512, name="att_in_proj")
    cos_t, sin_a, sin_b = _rope_tables(jnp.arange(seq, dtype=jnp.int32))
    o, k_rot, v = _moba_prompt(proj, cos_t, sin_a, sin_b, batch=batch, seq=seq)
    y = _matmul(o, w_out, res=x, tm=512, name="att_out_proj")
    return y, k_rot, v


def _moba_layer_step(x, gain, cache_k, cache_v, page_table, w_in, w_out):
    b = x.shape[0]
    hd = HEAD_DIM
    n_pool, page = cache_k.shape[0], cache_k.shape[1]
    past_len = page_table.shape[1] * page
    own = past_len // MOBA_BLOCK
    assert own * MOBA_BLOCK == past_len and own >= 1
    n_sel = min(MOBA_TOPK, own)
    ppb = MOBA_BLOCK // page
    proj = _matmul(x, w_in, gain=gain, hp=True, tn=1024, name="att_in_proj_s")
    ck = cache_k.reshape(n_pool, page * KV_HEADS, hd)
    cv = cache_v.reshape(n_pool, page * KV_HEADS, hd)
    kmeans = _cache_block_means(ck, page_table)
    cos_t, sin_a, sin_b = _rope_tables(jnp.full((1,), past_len, jnp.int32))
    q, k_rot, sel = _moba_select(proj, kmeans, cos_t, sin_a, sin_b, n_sel=n_sel)
    logical = sel[:, :, :n_sel, None] * ppb + jnp.arange(ppb, dtype=jnp.int32)
    pages = jnp.take_along_axis(page_table, logical.reshape(b, -1), axis=1).reshape(-1)
    v_new = proj[:, (ATT_HEADS + KV_HEADS) * hd:]
    o = _moba_step(q, k_rot.reshape(b, KV_HEADS, hd), v_new.reshape(b, KV_HEADS, hd), ck, cv, pages)
    y = _matmul(o, w_out, res=x, hp=True, name="att_out_proj_s")
    return y, k_rot.reshape(b, KV_HEADS * hd), v_new


def _norm_kernel(x_ref, g_ref, o_ref):
    o_ref[...] = _rms(x_ref[...], g_ref[...])


def _final_norm(x, gain, *, tm=512):
    m, d = x.shape
    tm = min(tm, m)
    return pl.pallas_call(
        _norm_kernel,
        out_shape=jax.ShapeDtypeStruct((m, d), F32),
        grid=(m // tm,),
        in_specs=[pl.BlockSpec((tm, d), lambda i: (i, 0)), pl.BlockSpec((1, d), lambda i: (0, 0))],
        out_specs=pl.BlockSpec((tm, d), lambda i: (i, 0)),
        compiler_params=_params("parallel"),
        name="final_norm",
    )(x, gain.reshape(1, d))


def kernel(x_prompt, x_sample, state_conv, state_delta, cache_k, cache_v, page_table, norm_mix, norm_ffn,
           norm_final, gdn_w_in, gdn_conv_w, gdn_a_log, gdn_dt_bias, gdn_o_norm, gdn_w_out, att_w_in,
           att_w_out, ffn_w_gate, ffn_w_up, ffn_w_down, moe_router, moe_w_gate, moe_w_up, moe_w_down):
    bp, seq, d = x_prompt.shape
    db, dec_seq, _ = x_sample.shape
    assert dec_seq == 1
    depth = norm_mix.shape[0]
    hp = x_prompt.reshape(bp * seq, d)
    hs = x_sample.reshape(db, d)
    conv_p, delta_p, k_p, v_p = [], [], [], []
    conv_s, delta_s, k_s, v_s = [], [], [], []
    for i in range(depth):
        j = i // 2
        last = i == depth - 1
        if i % 2 == 0:
            gdn = (gdn_w_in[j], gdn_conv_w[j], gdn_a_log[j], gdn_dt_bias[j], gdn_o_norm[j], gdn_w_out[j])
            hp, cp, sp = _gdn_layer_prompt(hp, norm_mix[i], *gdn, batch=bp, seq=seq)
            hs, cs, ss = _gdn_layer_step(hs, norm_mix[i], state_conv[j], state_delta[j], *gdn)
            conv_p.append(cp)
            delta_p.append(sp.astype(state_delta.dtype))
            conv_s.append(cs)
            delta_s.append(ss.astype(state_delta.dtype))
            ffn = (norm_ffn[i], ffn_w_gate[j], ffn_w_up[j], ffn_w_down[j])
            hp = _ffn(hp, *ffn)
            hs = _ffn(hs, *ffn, hp=True)
            if last:
                hp = _final_norm(hp, norm_final)
                hs = _final_norm(hs, norm_final)
        else:
            hp, kp, vp = _moba_layer_prompt(hp, norm_mix[i], att_w_in[j], att_w_out[j], batch=bp, seq=seq)
            hs, kn, vn = _moba_layer_step(hs, norm_mix[i], cache_k[j], cache_v[j], page_table,
                                          att_w_in[j], att_w_out[j])
            k_p.append(kp.reshape(bp, seq, KV_HEADS, HEAD_DIM))
            v_p.append(vp.reshape(bp, seq, KV_HEADS, HEAD_DIM))
            k_s.append(kn.reshape(db, 1, KV_HEADS, HEAD_DIM))
            v_s.append(vn.reshape(db, 1, KV_HEADS, HEAD_DIM))
            moe = (norm_ffn[i], moe_router[j], moe_w_gate[j], moe_w_up[j], moe_w_down[j])
            fg = norm_final if last else None
            hp = _moe_layer(hp, *moe, fg, tm=512, tf=1792)
            hs = _moe_layer(hs, *moe, fg, tm=32, tf=1792 if last else 512, hp=not last)
    return (hp.reshape(bp, seq, d), hs.reshape(db, 1, d),
            jnp.stack(conv_p), jnp.stack(delta_p), jnp.stack(k_p), jnp.stack(v_p),
            jnp.stack(conv_s), jnp.stack(delta_s), jnp.stack(k_s), jnp.stack(v_s))
```
